```python
import math
import jax, jax.numpy as jnp
from jax import lax
import numpy as np

D_MODEL = 4096
BATCH = 2
SEQ = 4096
DEPTH = 2

N_A_LAYERS = DEPTH // 2
N_B_LAYERS = DEPTH - N_A_LAYERS

MLSTM_HEADS = 8
MLSTM_QK_DIM = D_MODEL // MLSTM_HEADS // 2
MLSTM_V_DIM = D_MODEL // MLSTM_HEADS
MLSTM_CHUNK = 64
MLSTM_IN_WIDTH = 2 * MLSTM_HEADS * MLSTM_QK_DIM + 2 * MLSTM_HEADS * MLSTM_V_DIM + 2 * MLSTM_HEADS

ATT_GROUPS = ((128, 1), (512, 4), (2048, 16))
N_ATT_GROUPS = len(ATT_GROUPS)
ATT_HEADS = 32
HEAD_DIM = D_MODEL // ATT_HEADS
ATT_WIDTH = N_ATT_GROUPS * ATT_HEADS * HEAD_DIM
ATT_BLOCK = 128
ROPE_DIM = HEAD_DIM // 4
ROPE_THETA = 500000.0

N_EXPERTS = 32
N_EXPERT_GROUPS = 8
EXPERTS_PER_GROUP = N_EXPERTS // N_EXPERT_GROUPS
TOP_K = 2
D_EXPERT = 3 * D_MODEL // 16

DEEPNORM_ALPHA = (2.0 * DEPTH) ** 0.25
DEEPNORM_BETA = (8.0 * DEPTH) ** -0.25
NORM_EPS = 1e-5

kernel_name = "yoco_mlstm_dilated_attn_grouped_moe"


def layer_norm(x, g, b):
    xf = x.astype(jnp.float32)
    mu = jnp.mean(xf, axis=-1, keepdims=True)
    var = jnp.mean(jnp.square(xf - mu), axis=-1, keepdims=True)
    y = (xf - mu) * lax.rsqrt(var + NORM_EPS) * g.astype(jnp.float32) + b.astype(jnp.float32)
    return y.astype(x.dtype)


def rotary_partial(x, positions):
    half = ROPE_DIM // 2
    inv_freq = ROPE_THETA ** (-jnp.arange(half, dtype=jnp.float32) / half)
    ang = positions.astype(jnp.float32)[:, None] * inv_freq[None, :]
    cos = jnp.cos(ang)[None, :, None, :]
    sin = jnp.sin(ang)[None, :, None, :]
    xr = x[..., :ROPE_DIM].astype(jnp.float32)
    x1, x2 = xr[..., :half], xr[..., half:]
    rot = jnp.concatenate([x1 * cos - x2 * sin, x1 * sin + x2 * cos], axis=-1)
    return jnp.concatenate([rot.astype(x.dtype), x[..., ROPE_DIM:]], axis=-1)


def mlstm_mixer(x, w_in, b_gate, norm_g, w_out):
    B, S, _ = x.shape
    H, dk, dv, CH = MLSTM_HEADS, MLSTM_QK_DIM, MLSTM_V_DIM, MLSTM_CHUNK
    nc = S // CH
    proj = x @ w_in
    c1 = H * dk; c2 = 2 * H * dk; c3 = c2 + H * dv; c4 = c3 + H * dv
    q, k, v, o_pre, gates = proj[..., :c1], proj[..., c1:c2], proj[..., c2:c3], proj[..., c3:c4], proj[..., c4:]
    gates = gates.astype(jnp.float32) + b_gate.astype(jnp.float32)
    log_i = gates[..., :H]
    log_f = jax.nn.log_sigmoid(gates[..., H:])

    def to_chunks(t, d):
        return t.reshape(B, nc, CH, H, d).transpose(1, 0, 3, 2, 4).astype(jnp.float32)

    def gate_chunks(t):
        return t.reshape(B, nc, CH, H).transpose(1, 0, 3, 2)

    qc = to_chunks(q, dk) * (dk ** -0.5)
    kc = to_chunks(k, dk)
    vc = to_chunks(v, dv)
    lic, lfc = gate_chunks(log_i), gate_chunks(log_f)
    causal = jnp.tril(jnp.ones((CH, CH), dtype=bool))

    def step(carry, inp):
        C, n, m = carry
        qq, kk, vv, li, lf = inp
        b = jnp.cumsum(lf, axis=-1)
        dmat = jnp.where(causal, b[..., :, None] - b[..., None, :] + li[..., None, :], -jnp.inf)
        m_inter = b + m[..., None]
        m_t = jnp.maximum(m_inter, jnp.max(dmat, axis=-1))
        s = jnp.einsum('bhtk,bhsk->bhts', qq, kk) * jnp.exp(dmat - m_t[..., None])
        inter = jnp.exp(m_inter - m_t)
        num = inter[..., None] * jnp.einsum('bhtk,bhkv->bhtv', qq, C) + jnp.einsum('bhts,bhsv->bhtv', s, vv)
        den = inter * jnp.einsum('bhtk,bhk->bht', qq, n) + jnp.sum(s, axis=-1)
        h = num / jnp.maximum(jnp.abs(den), jnp.exp(-m_t))[..., None]
        b_last = b[..., -1]
        g = b_last[..., None] - b + li
        m_new = jnp.maximum(b_last + m, jnp.max(g, axis=-1))
        decay = jnp.exp(b_last + m - m_new)
        w = jnp.exp(g - m_new[..., None])
        wk = w[..., None] * kk
        C_new = decay[..., None, None] * C + jnp.einsum('bhsk,bhsv->bhkv', wk, vv)
        n_new = decay[..., None] * n + jnp.sum(wk, axis=-2)
        return (C_new, n_new, m_new), h

    init = (jnp.zeros((B, H, dk, dv), jnp.float32), jnp.zeros((B, H, dk), jnp.float32),
            jnp.zeros((B, H), jnp.float32))
    _, hs = lax.scan(step, init, (qc, kc, vc, lic, lfc))
    h = hs.transpose(1, 0, 3, 2, 4).reshape(B, S, H, dv)
    h = h * lax.rsqrt(jnp.mean(jnp.square(h), axis=-1, keepdims=True) + NORM_EPS)
    h = h.reshape(B, S, H * dv) * norm_g.astype(jnp.float32)
    h = h * jax.nn.sigmoid(o_pre.astype(jnp.float32))
    return h.astype(x.dtype) @ w_out


def shared_kv(h, w_kv, positions):
    B, S, _ = h.shape
    kv = (h @ w_kv).reshape(B, S, 2, N_ATT_GROUPS * ATT_HEADS, HEAD_DIM)
    k = rotary_partial(kv[:, :, 0], positions)
    v = kv[:, :, 1]
    return (k.reshape(B, S, N_ATT_GROUPS, ATT_HEADS, HEAD_DIM),
            v.reshape(B, S, N_ATT_GROUPS, ATT_HEADS, HEAD_DIM))


def dilated_branch(q, k, v, window, dilation):
    B, S, H, Dh = q.shape
    L = S // dilation
    span = window // dilation
    nb = -(-L // ATT_BLOCK)
    Lp = nb * ATT_BLOCK

    def blocks(t):
        t = t.reshape(B, L, dilation, H, Dh)
        t = jnp.pad(t, ((0, 0), (0, Lp - L), (0, 0), (0, 0), (0, 0)))
        return t.reshape(B, nb, ATT_BLOCK, dilation, H, Dh)

    def with_prev(t):
        prev = jnp.pad(t, ((0, 0), (1, 0), (0, 0), (0, 0), (0, 0), (0, 0)))[:, :-1]
        return jnp.concatenate([prev, t], axis=2)

    qb = blocks(q)
    kk, vv = with_prev(blocks(k)), with_prev(blocks(v))
    s = jnp.einsum('bnirhc,bnjrhc->bnrhij', qb, kk, preferred_element_type=jnp.float32) * (Dh ** -0.5)
    i = jnp.arange(ATT_BLOCK)[:, None]
    j = jnp.arange(2 * ATT_BLOCK)[None, :]
    dist = ATT_BLOCK + i - j
    key_pos = (jnp.arange(nb)[:, None, None] - 1) * ATT_BLOCK + j[None]
    valid = (dist >= 0)[None] & (dist <= span)[None] & (key_pos >= 0)
    s = jnp.where(valid[None, :, None, None], s, -jnp.inf)
    lse = jax.nn.logsumexp(s, axis=-1)
    p = jnp.exp(s - lse[..., None])
    o = jnp.einsum('bnrhij,bnjrhc->bnirhc', p, vv.astype(jnp.float32))
    o = o.reshape(B, Lp, dilation, H, Dh)[:, :L].reshape(B, S, H, Dh)
    lse = lse.transpose(0, 1, 4, 2, 3).reshape(B, Lp, dilation, H)[:, :L].reshape(B, S, H)
    return o, lse


def dilated_mixer(x, w_q, k_sh, v_sh, w_o, positions):
    B, S, _ = x.shape
    q = (x @ w_q).reshape(B, S, N_ATT_GROUPS * ATT_HEADS, HEAD_DIM)
    q = rotary_partial(q, positions).reshape(B, S, N_ATT_GROUPS, ATT_HEADS, HEAD_DIM)
    outs, lses = [], []
    for g, (window, dilation) in enumerate(ATT_GROUPS):
        o, l = dilated_branch(q[:, :, g], k_sh[:, :, g], v_sh[:, :, g], window, dilation)
        outs.append(o)
        lses.append(l)
    mix = jax.nn.softmax(jnp.stack(lses, axis=0), axis=0)
    o = jnp.sum(mix[..., None] * jnp.stack(outs, axis=0), axis=0)
    return o.reshape(B, S, ATT_HEADS * HEAD_DIM).astype(x.dtype) @ w_o


def grouped_moe(x, w_router, b_router, w_gate, w_up, w_down):
    B, S, D = x.shape
    t = x.reshape(B * S, D)
    logits = (t @ w_router).astype(jnp.float32) + b_router.astype(jnp.float32)
    probs = jax.nn.softmax(logits, axis=-1)
    grouped = probs.reshape(-1, N_EXPERT_GROUPS, EXPERTS_PER_GROUP)
    group_score = jnp.sum(lax.top_k(grouped, TOP_K)[0], axis=-1)
    best_group = jnp.argmax(group_score, axis=-1)
    in_group = (jnp.arange(N_EXPERTS) // EXPERTS_PER_GROUP)[None, :] == best_group[:, None]
    top_val, top_idx = lax.top_k(jnp.where(in_group, probs, -1.0), TOP_K)
    top_w = top_val / jnp.sum(top_val, axis=-1, keepdims=True)
    gates = jnp.einsum('tk,tke->te', top_w, jax.nn.one_hot(top_idx, N_EXPERTS, dtype=jnp.float32))
    h = jax.nn.silu(jnp.einsum('td,edf->tef', t, w_gate)) * jnp.einsum('td,edf->tef', t, w_up)
    h = h * gates[:, :, None].astype(h.dtype)
    y = jnp.einsum('tef,efd->td', h, w_down)
    return y.reshape(B, S, D)


def setup_inputs(seed: int = 0) -> dict:
    key = jax.random.key(seed)
    ks = jax.random.split(key, 16)
    D = D_MODEL
    f32 = jnp.float32

    def nrm(k, shape, scale):
        return jax.random.normal(k, shape, f32) * scale

    H = MLSTM_HEADS
    f_bias = jnp.broadcast_to(jnp.linspace(3.0, 6.0, H, dtype=f32), (N_A_LAYERS, H))
    b_gate = jnp.concatenate([nrm(ks[2], (N_A_LAYERS, H), 0.1),
                              f_bias + nrm(ks[3], (N_A_LAYERS, H), 0.1)], axis=-1)
    return {
        "x": nrm(ks[0], (BATCH, SEQ, D), 1.0),
        "mlstm_w_in": nrm(ks[1], (N_A_LAYERS, D, MLSTM_IN_WIDTH), D ** -0.5),
        "mlstm_b_gate": b_gate,
        "mlstm_norm_g": 1.0 + nrm(ks[4], (N_A_LAYERS, H * MLSTM_V_DIM), 0.02),
        "mlstm_w_out": nrm(ks[5], (N_A_LAYERS, H * MLSTM_V_DIM, D), DEEPNORM_BETA * (H * MLSTM_V_DIM) ** -0.5),
        "att_w_q": nrm(ks[6], (N_B_LAYERS, D, ATT_WIDTH), D ** -0.5),
        "att_w_kv_shared": nrm(ks[7], (D, 2 * ATT_WIDTH), D ** -0.5),
        "att_w_o": nrm(ks[8], (N_B_LAYERS, ATT_HEADS * HEAD_DIM, D), DEEPNORM_BETA * (ATT_HEADS * HEAD_DIM) ** -0.5),
        "router_w": nrm(ks[9], (D, N_EXPERTS), D ** -0.5),
        "router_b": nrm(ks[10], (N_EXPERTS,), 0.01),
        "expert_w_gate": nrm(ks[11], (DEPTH, N_EXPERTS, D, D_EXPERT), D ** -0.5),
        "expert_w_up": nrm(ks[12], (DEPTH, N_EXPERTS, D, D_EXPERT), D ** -0.5),
        "expert_w_down": nrm(ks[13], (DEPTH, N_EXPERTS, D_EXPERT, D), DEEPNORM_BETA * D_EXPERT ** -0.5),
        "ln_g": 1.0 + nrm(ks[14], (DEPTH, 2, D), 0.02),
        "ln_b": nrm(ks[15], (DEPTH, 2, D), 0.02),
    }


def reference(x, mlstm_w_in, mlstm_b_gate, mlstm_norm_g, mlstm_w_out, att_w_q, att_w_kv_shared,
              att_w_o, router_w, router_b, expert_w_gate, expert_w_up, expert_w_down, ln_g, ln_b):
    positions = jnp.arange(x.shape[1])
    h = x
    k_sh = v_sh = None
    for layer in range(DEPTH):
        if layer < N_A_LAYERS:
            mix = mlstm_mixer(h, mlstm_w_in[layer], mlstm_b_gate[layer], mlstm_norm_g[layer], mlstm_w_out[layer])
        else:
            if layer == N_A_LAYERS:
                k_sh, v_sh = shared_kv(h, att_w_kv_shared, positions)
            bl = layer - N_A_LAYERS
            mix = dilated_mixer(h, att_w_q[bl], k_sh, v_sh, att_w_o[bl], positions)
        h = layer_norm(DEEPNORM_ALPHA * h + mix, ln_g[layer, 0], ln_b[layer, 0])
        moe_out = grouped_moe(h, router_w, router_b, expert_w_gate[layer], expert_w_up[layer], expert_w_down[layer])
        h = layer_norm(DEEPNORM_ALPHA * h + moe_out, ln_g[layer, 1], ln_b[layer, 1])
    return h
```

```python
import functools
import math

import jax
import jax.numpy as jnp
from jax import lax
from jax.experimental import pallas as pl
from jax.experimental.pallas import tpu as pltpu

F32 = jnp.float32
BF16 = jnp.bfloat16

HEAD_DIM = 128
ATT_GROUPS = ((128, 1), (512, 4), (2048, 16))
ATT_BLOCK = 128
ROPE_DIM = HEAD_DIM // 4
ROPE_THETA = 500000.0
N_EXPERT_GROUPS = 8
EXPERTS_PER_GROUP = 4
NORM_EPS = 1e-5
MLSTM_CHUNK = 256

LANES = 128
SUBLANES = 8
VMEM_LIMIT = 56 * 1024 * 1024
NEG = -1e30


def _cparams(*sem):
    return pltpu.CompilerParams(dimension_semantics=sem, vmem_limit_bytes=VMEM_LIMIT)


def _mm_body(x_ref, w_ref, o_ref):
    o_ref[...] = jnp.dot(x_ref[...], w_ref[...].astype(BF16),
                         preferred_element_type=F32).astype(o_ref.dtype)


def _mm(x, w, layer, col0, ncols, out_dtype, tm=1024, tn=512):
    M, K = x.shape
    tm = min(tm, M)
    assert M % tm == 0 and ncols % tn == 0 and col0 % tn == 0
    c0 = col0 // tn
    return pl.pallas_call(
        _mm_body,
        grid=(M // tm, ncols // tn),
        in_specs=[pl.BlockSpec((tm, K), lambda i, j: (i, 0)),
                  pl.BlockSpec((None, K, tn), lambda i, j: (layer, 0, c0 + j))],
        out_specs=pl.BlockSpec((tm, tn), lambda i, j: (i, j)),
        out_shape=jax.ShapeDtypeStruct((M, ncols), out_dtype),
        compiler_params=_cparams("parallel", "parallel"),
        name="dense_proj",
    )(x, w)


def _gates_body(wt_ref, x_ref, b_ref, o_ref, *, n_heads):
    g = lax.dot_general(wt_ref[...], x_ref[...], (((1,), (1,)), ((), ())),
                        precision=lax.Precision.HIGHEST, preferred_element_type=F32)
    g = g + b_ref[...]
    log_sig = jnp.minimum(g, 0.0) - jnp.log1p(jnp.exp(-jnp.abs(g)))
    row = lax.broadcasted_iota(jnp.int32, g.shape, 0)
    o_ref[...] = jnp.where(row >= n_heads, log_sig, g)


def _mlstm_gates(x, w_gates_t, b_gate, tm=512):
    T, D = x.shape
    H2 = w_gates_t.shape[0]
    tm = min(tm, T)
    return pl.pallas_call(
        functools.partial(_gates_body, n_heads=H2 // 2),
        grid=(T // tm,),
        in_specs=[pl.BlockSpec((H2, D), lambda i: (0, 0)),
                  pl.BlockSpec((tm, D), lambda i: (i, 0)),
                  pl.BlockSpec((H2, 1), lambda i: (0, 0))],
        out_specs=pl.BlockSpec((H2, tm), lambda i: (0, i)),
        out_shape=jax.ShapeDtypeStruct((H2, T), F32),
        compiler_params=_cparams("parallel"),
        name="mlstm_gates",
    )(w_gates_t, x, b_gate.reshape(H2, 1).astype(F32))


def _mlstm_body(q_ref, k_ref, v_ref, op_ref, li_ref, lf_ref, g_ref, out_ref,
                c_ref, n_ref, m_ref, *, scale):
    L = q_ref.shape[0]

    @pl.when(pl.program_id(2) == 0)
    def _():
        c_ref[...] = jnp.zeros_like(c_ref)
        n_ref[...] = jnp.zeros_like(n_ref)
        m_ref[...] = jnp.zeros_like(m_ref)

    q = q_ref[...]
    k = k_ref[...]
    v = v_ref[...]
    li = li_ref[...]
    lf = lf_ref[...]
    m_prev = m_ref[...]

    row = lax.broadcasted_iota(jnp.int32, (L, L), 0)
    col = lax.broadcasted_iota(jnp.int32, (L, L), 1)
    causal = col <= row
    eye = col == row

    def to_col(r):
        return jnp.sum(jnp.where(eye, jnp.broadcast_to(r, (L, L)), 0.0), axis=1, keepdims=True)

    b_col = jnp.sum(jnp.where(causal, jnp.broadcast_to(lf, (L, L)), 0.0), axis=1, keepdims=True)
    b_row = jnp.sum(jnp.where(eye, jnp.broadcast_to(b_col, (L, L)), 0.0), axis=0, keepdims=True)
    a_row = li - b_row
    a_b = jnp.broadcast_to(a_row, (L, L))
    mloc = jnp.maximum(m_prev, jnp.max(jnp.where(causal, a_b, NEG), axis=1, keepdims=True))
    dmat = jnp.exp(jnp.where(causal, a_b - mloc, NEG))
    inter = jnp.exp(m_prev - mloc)

    s = lax.dot_general(q, k, (((1,), (1,)), ((), ())), preferred_element_type=F32) * scale * dmat
    qc = jnp.dot(q, c_ref[...].astype(BF16), preferred_element_type=F32) * scale
    num = inter * qc + jnp.dot(s.astype(BF16), v, preferred_element_type=F32)
    qn = jnp.sum(q.astype(F32) * n_ref[...], axis=1, keepdims=True) * scale
    den = inter * qn + jnp.sum(s, axis=1, keepdims=True)
    m_t = b_col + mloc
    h = num / jnp.maximum(jnp.abs(den), jnp.exp(-m_t))

    m_last = jnp.maximum(m_prev, jnp.max(a_row, axis=1, keepdims=True))
    b_last = jnp.sum(lf, axis=1, keepdims=True)
    decay = jnp.exp(m_prev - m_last)
    w_col = to_col(jnp.exp(a_row - m_last))
    wk = w_col * k.astype(F32)
    c_ref[...] = decay * c_ref[...] + lax.dot_general(
        wk.astype(BF16), v, (((0,), (0,)), ((), ())), preferred_element_type=F32)
    n_ref[...] = decay * n_ref[...] + jnp.sum(wk, axis=0, keepdims=True)
    m_ref[...] = b_last + m_last

    hn = h * lax.rsqrt(jnp.mean(h * h, axis=1, keepdims=True) + NORM_EPS)
    gate = 1.0 / (1.0 + jnp.exp(-op_ref[...].astype(F32)))
    out_ref[...] = (hn * g_ref[...] * gate).astype(out_ref.dtype)


def _mlstm(proj, gates_t, norm_g, B, S, H, dk, dv):
    L = min(MLSTM_CHUNK, S)
    nc = S // L
    assert S % L == 0 and (2 * H * dk) % dv == 0
    v0 = (2 * H * dk) // dv
    gates3 = gates_t.reshape(2 * H, 1, B * S)
    return pl.pallas_call(
        functools.partial(_mlstm_body, scale=dk ** -0.5),
        grid=(B, H, nc),
        in_specs=[pl.BlockSpec((None, L, dk), lambda b, h, c: (b, c, h)),
                  pl.BlockSpec((None, L, dk), lambda b, h, c: (b, c, H + h)),
                  pl.BlockSpec((None, L, dv), lambda b, h, c: (b, c, v0 + h)),
                  pl.BlockSpec((None, L, dv), lambda b, h, c: (b, c, v0 + H + h)),
                  pl.BlockSpec((None, 1, L), lambda b, h, c: (h, 0, b * nc + c)),
                  pl.BlockSpec((None, 1, L), lambda b, h, c: (H + h, 0, b * nc + c)),
                  pl.BlockSpec((1, dv), lambda b, h, c: (0, h))],
        out_specs=pl.BlockSpec((None, L, dv), lambda b, h, c: (b, c, h)),
        out_shape=jax.ShapeDtypeStruct((B, S, H * dv), BF16),
        scratch_shapes=[pltpu.VMEM((dk, dv), F32), pltpu.VMEM((1, dk), F32), pltpu.VMEM((1, 1), F32)],
        compiler_params=_cparams("parallel", "parallel", "arbitrary"),
        name="mlstm_chunks",
    )(proj, proj, proj, proj, gates3, gates3, norm_g.reshape(1, H * dv).astype(F32))


def _slab_pitch(D):
    return D // LANES + SUBLANES


def _slab_load(ref, n_tok, D):
    pitch = _slab_pitch(D)
    return jnp.concatenate([ref[pl.ds(c, n_tok, stride=pitch), :] for c in range(D // LANES)], axis=1)


def _slab_store(ref, val, pad=True):
    n_tok, D = val.shape
    pitch = _slab_pitch(D)
    nrow = D // LANES
    for c in range(nrow):
        ref[pl.ds(c, n_tok, stride=pitch), :] = val[:, c * LANES:(c + 1) * LANES].astype(ref.dtype)
    if pad:
        for c in range(nrow, pitch):
            ref[pl.ds(c, n_tok, stride=pitch), :] = jnp.zeros((n_tok, LANES), ref.dtype)


def _ln_body(*refs, tm, slabs, alpha, router, outs):
    n_in = len(slabs)
    D = refs[n_in].shape[1]
    vals = [_slab_load(r, tm, D) if sl else r[...].astype(F32) for r, sl in zip(refs[:n_in], slabs)]
    g_ref, b_ref = refs[n_in], refs[n_in + 1]
    pos = n_in + 2
    if router:
        wr_ref, br_ref = refs[pos], refs[pos + 1]
        pos += 2

    z = alpha * vals[0]
    for a in vals[1:]:
        z = z + a
    mu = jnp.mean(z, axis=1, keepdims=True)
    zc = z - mu
    var = jnp.mean(zc * zc, axis=1, keepdims=True)
    y = zc * lax.rsqrt(var + NORM_EPS) * g_ref[...] + b_ref[...]
    for kind in outs:
        if kind == "slab":
            _slab_store(refs[pos], y)
        else:
            refs[pos][...] = y.astype(refs[pos].dtype)
        pos += 1

    if router:
        ei_ref, ew_ref = refs[pos], refs[pos + 1]
        ng, epg = N_EXPERT_GROUPS, EXPERTS_PER_GROUP
        lt = lax.dot_general(wr_ref[...], y, (((1,), (1,)), ((), ())),
                             precision=lax.Precision.HIGHEST, preferred_element_type=F32) + br_ref[...]
        ex = jnp.exp(lt - jnp.max(lt, axis=0, keepdims=True))
        p = ex / jnp.sum(ex, axis=0, keepdims=True)
        p0, p1, p2, p3 = [p[j * ng:(j + 1) * ng] for j in range(epg)]
        a, b = jnp.maximum(p0, p1), jnp.minimum(p0, p1)
        c, d = jnp.maximum(p2, p3), jnp.minimum(p2, p3)
        score = jnp.maximum(a, c) + jnp.maximum(jnp.minimum(a, c), jnp.maximum(b, d))
        gi = lax.broadcasted_iota(jnp.int32, score.shape, 0)
        best = jnp.min(jnp.where(score == jnp.max(score, axis=0, keepdims=True), gi, ng),
                       axis=0, keepdims=True)
        sel = gi == best
        v = [jnp.sum(jnp.where(sel, pj, 0.0), axis=0, keepdims=True) for pj in (p0, p1, p2, p3)]

        def first_max(vals):
            top = jnp.maximum(jnp.maximum(vals[0], vals[1]), jnp.maximum(vals[2], vals[3]))
            idx = jnp.where(vals[0] == top, 0, jnp.where(vals[1] == top, 1, jnp.where(vals[2] == top, 2, 3)))
            return top, idx

        v1, j1 = first_max(v)
        v2, j2 = first_max([jnp.where(j1 == j, -1.0, v[j]) for j in range(epg)])
        tot = v1 + v2
        ei_ref[...] = jnp.concatenate([best * epg + j1, best * epg + j2], axis=0).astype(jnp.int32)
        ew_ref[...] = jnp.concatenate([v1 / tot, v2 / tot], axis=0)


def _ln(T, D, inputs, g, b, alpha, outs, router=None, tm=256):
    tm = min(tm, T)
    pitch = _slab_pitch(D)

    def spec(is_slab, off):
        if is_slab:
            return pl.BlockSpec((tm * pitch, LANES), lambda i: (off + i, 0))
        return pl.BlockSpec((tm, D), lambda i: (off + i, 0))

    in_specs = [spec(sl, off) for _, sl, off in inputs] + [pl.BlockSpec((1, D), lambda i: (0, 0))] * 2
    args = [a for a, _, _ in inputs] + [g.reshape(1, D).astype(F32), b.reshape(1, D).astype(F32)]
    out_shape, out_specs = [], []
    for kind in outs:
        if kind == "slab":
            out_shape.append(jax.ShapeDtypeStruct((T * pitch, LANES), F32))
        else:
            out_shape.append(jax.ShapeDtypeStruct((T, D), F32 if kind == "f32" else BF16))
        out_specs.append(spec(kind == "slab", 0))
    if router is not None:
        wr_t, br = router
        E = wr_t.shape[0]
        in_specs += [pl.BlockSpec((E, D), lambda i: (0, 0)), pl.BlockSpec((E, 1), lambda i: (0, 0))]
        args += [wr_t, br]
        out_shape += [jax.ShapeDtypeStruct((2, T), jnp.int32), jax.ShapeDtypeStruct((2, T), F32)]
        out_specs += [pl.BlockSpec((2, tm), lambda i: (0, i))] * 2
    return pl.pallas_call(
        functools.partial(_ln_body, tm=tm, slabs=tuple(sl for _, sl, _ in inputs), alpha=alpha,
                          router=router is not None, outs=tuple(outs)),
        grid=(T // tm,),
        in_specs=in_specs, out_specs=out_specs, out_shape=out_shape,
        compiler_params=_cparams("parallel"),
        name="deepnorm_ln_router" if router is not None else "deepnorm_ln",
    )(*args)


def _ffn_body(te_ref, nu_ref, tok_ref, dst_ref, h_hbm, rw_ref, wg_ref, wu_ref, wd_ref, y_hbm,
              slab_ref, xb_ref, acc_ref, sem_in, sem_out, *, tm, D):
    i, j = pl.program_id(0), pl.program_id(1)
    pitch = _slab_pitch(D)

    def slab_row(r):
        return slab_ref.at[pl.ds(pl.multiple_of(r * pitch, SUBLANES), pitch)]

    @pl.when(i < nu_ref[0])
    def _():
        @pl.when(j == 0)
        def _():
            def fetch(r, carry):
                pltpu.make_async_copy(h_hbm.at[tok_ref[i * tm + r]], slab_row(r), sem_in).start()
                return carry

            def fetched(r, carry):
                pltpu.make_async_copy(h_hbm.at[0], slab_row(r), sem_in).wait()
                return carry

            lax.fori_loop(0, tm, fetch, 0)
            lax.fori_loop(0, tm, fetched, 0)
            xb_ref[...] = _slab_load(slab_ref, tm, D).astype(BF16)

        x = xb_ref[...]
        g = jnp.dot(x, wg_ref[...].astype(BF16), preferred_element_type=F32)
        u = jnp.dot(x, wu_ref[...].astype(BF16), preferred_element_type=F32)
        hmid = g * (1.0 / (1.0 + jnp.exp(-g))) * u * rw_ref[...]
        part = jnp.dot(hmid.astype(BF16), wd_ref[...].astype(BF16), preferred_element_type=F32)

        @pl.when(j == 0)
        def _():
            acc_ref[...] = part

        @pl.when(j > 0)
        def _():
            acc_ref[...] += part

        @pl.when(j == pl.num_programs(1) - 1)
        def _():
            _slab_store(slab_ref, acc_ref[...], pad=False)

            def send(r, carry):
                dst = dst_ref[i * tm + r]

                @pl.when(dst >= 0)
                def _():
                    pltpu.make_async_copy(slab_row(r), y_hbm.at[dst], sem_out).start()
                return carry

            def sent(r, carry):
                @pl.when(dst_ref[i * tm + r] >= 0)
                def _():
                    pltpu.make_async_copy(slab_row(r), y_hbm.at[0], sem_out).wait()
                return carry

            lax.fori_loop(0, tm, send, 0)
            lax.fori_loop(0, tm, sent, 0)


def _expert_ffn(h_slab, row_token, row_dst, row_w, tile_expert, n_used, w_gate, w_up, w_down, layer,
                T, D, tm, fc):
    pitch = _slab_pitch(D)
    P = row_token.shape[0]
    F = w_gate.shape[-1]
    assert P % tm == 0 and F % fc == 0

    def row_map(i, j, te, nu, tok, dst):
        return (jnp.minimum(i, nu[0] - 1), 0)

    def w_map(i, j, te, nu, tok, dst):
        return (layer, te[i], 0, j)

    out = pl.pallas_call(
        functools.partial(_ffn_body, tm=tm, D=D),
        grid_spec=pltpu.PrefetchScalarGridSpec(
            num_scalar_prefetch=4, grid=(P // tm, F // fc),
            in_specs=[pl.BlockSpec(memory_space=pl.ANY),
                      pl.BlockSpec((tm, 1), row_map),
                      pl.BlockSpec((None, None, D, fc), w_map),
                      pl.BlockSpec((None, None, D, fc), w_map),
                      pl.BlockSpec((None, None, fc, D), lambda i, j, te, nu, tok, dst: (layer, te[i], j, 0))],
            out_specs=pl.BlockSpec(memory_space=pl.ANY),
            scratch_shapes=[pltpu.VMEM((tm * pitch, LANES), F32), pltpu.VMEM((tm, D), BF16),
                            pltpu.VMEM((tm, D), F32), pltpu.SemaphoreType.DMA(()),
                            pltpu.SemaphoreType.DMA(())]),
        out_shape=jax.ShapeDtypeStruct((2 * T, pitch, LANES), F32),
        compiler_params=_cparams("arbitrary", "arbitrary"),
        name="expert_ffn",
    )(tile_expert, n_used, row_token, row_dst, h_slab.reshape(T, pitch, LANES), row_w, w_gate, w_up, w_down)
    return out.reshape(2 * T * pitch, LANES)


def _moe(h_slab, eidx, ew, w_gate, w_up, w_down, layer, T, D, tm=256):
    E, F = w_gate.shape[1], w_gate.shape[-1]
    A = 2 * T
    n_tiles = A // tm + E
    P = n_tiles * tm
    fc = 256 if F % 256 == 0 else F

    e_flat = eidx.reshape(A)
    w_flat = ew.reshape(A)
    order = jnp.argsort(e_flat, stable=True).astype(jnp.int32)
    sorted_e = e_flat[order]
    counts = jnp.sum((e_flat[:, None] == jnp.arange(E, dtype=jnp.int32)[None, :]).astype(jnp.int32), axis=0)
    tiles_e = (counts + tm - 1) // tm
    tile_end = jnp.cumsum(tiles_e)
    pad_start = (tile_end - tiles_e) * tm
    start = jnp.cumsum(counts) - counts
    dest = pad_start[sorted_e] + jnp.arange(A, dtype=jnp.int32) - start[sorted_e]
    row_token = jnp.zeros((P,), jnp.int32).at[dest].set(order % T)
    row_dst = jnp.full((P,), -1, jnp.int32).at[dest].set(order)
    row_w = jnp.zeros((P,), F32).at[dest].set(w_flat[order])
    n_used = tile_end[-1:].astype(jnp.int32)
    tile_ids = jnp.minimum(jnp.arange(n_tiles, dtype=jnp.int32), n_used[0] - 1)
    tile_expert = jnp.sum((tile_ids[:, None] >= tile_end[None, :]).astype(jnp.int32), axis=1).astype(jnp.int32)

    return _expert_ffn(h_slab, row_token, row_dst, row_w.reshape(P, 1), tile_expert, n_used,
                       w_gate, w_up, w_down, layer, T, D, tm, fc)


def _proj_heads_body(x_ref, w_ref, ca_ref, sb_ref, sc_ref, o_ref, acc_ref, *, d, hb, rope):
    acc = jnp.dot(x_ref[...], w_ref[...].astype(BF16), preferred_element_type=F32)
    tm = x_ref.shape[0]
    for hh in range(hb):
        acc_ref[hh * tm:(hh + 1) * tm, :] = acc[:, hh * HEAD_DIM:(hh + 1) * HEAD_DIM]
    n = tm // d
    half = ROPE_DIM // 2
    for r in range(d):
        if rope:
            rows = pl.ds(r, n, stride=d) if d > 1 else pl.ds(0, n)
            ca, sb, sc = ca_ref[rows, :], sb_ref[rows, :], sc_ref[rows, :]
        for hh in range(hb):
            val = acc_ref[pl.ds(hh * tm + r, n, stride=d) if d > 1 else pl.ds(hh * tm, n), :]
            if rope:
                val = (val * ca + pltpu.roll(val, half, 1) * sb
                       + pltpu.roll(val, HEAD_DIM - half, 1) * sc)
            o_ref[hh, r] = val.astype(o_ref.dtype)


def _proj_heads(x, w, layer, col0, n_heads, d, B, S, tables, rope, tm=1024, hb=4):
    M, K = x.shape
    tm = min(tm, S)
    tn = hb * HEAD_DIM
    assert S % tm == 0 and n_heads % hb == 0 and col0 % tn == 0 and (tm // d) % 16 == 0
    mt = S // tm
    c0 = col0 // tn
    tab_spec = pl.BlockSpec((tm, HEAD_DIM), lambda i, j: (i % mt, 0))
    return pl.pallas_call(
        functools.partial(_proj_heads_body, d=d, hb=hb, rope=rope),
        grid=(M // tm, n_heads // hb),
        in_specs=[pl.BlockSpec((tm, K), lambda i, j: (i, 0)),
                  pl.BlockSpec((None, K, tn), lambda i, j: (layer, 0, c0 + j)),
                  tab_spec, tab_spec, tab_spec],
        out_specs=pl.BlockSpec((None, hb, d, tm // d, HEAD_DIM), lambda i, j: (i // mt, j, 0, i % mt, 0)),
        out_shape=jax.ShapeDtypeStruct((B, n_heads, d, S // d, HEAD_DIM), BF16),
        scratch_shapes=[pltpu.VMEM((hb * tm, HEAD_DIM), F32)],
        compiler_params=_cparams("parallel", "parallel"),
        name="attn_proj_d%d" % d,
    )(x, w, *tables)


def _rope_tables(S, scale):
    half = ROPE_DIM // 2
    inv_freq = ROPE_THETA ** (-jnp.arange(half, dtype=F32) / half)
    ang = jnp.arange(S).astype(F32)[:, None] * inv_freq[None, :]
    cos, sin = jnp.cos(ang), jnp.sin(ang)
    zeros = jnp.zeros((S, HEAD_DIM - ROPE_DIM), F32)
    ca = jnp.concatenate([cos, cos, jnp.ones_like(zeros)], axis=1)
    sb = jnp.concatenate([jnp.zeros_like(sin), sin, zeros], axis=1)
    sc = jnp.concatenate([-sin, jnp.zeros_like(sin), zeros], axis=1)
    return ca * scale, sb * scale, sc * scale


def _attn_body(q_ref, kc_ref, kp_ref, vc_ref, vp_ref, o_ref, lse_ref, kbuf, vbuf):
    SB, LB, _ = q_ref.shape
    blk = ATT_BLOCK
    n = pl.program_id(1)
    kbuf[:, :blk] = kp_ref[...]
    kbuf[:, blk:] = kc_ref[...]
    vbuf[:, :blk] = vp_ref[...]
    vbuf[:, blk:] = vc_ref[...]

    row = lax.broadcasted_iota(jnp.int32, (blk, 2 * blk), 0)
    col = lax.broadcasted_iota(jnp.int32, (blk, 2 * blk), 1)
    band = (col >= row) & (col <= row + blk)
    band_first = band & ((col >= blk) | (n > 0))
    eye = (lax.broadcasted_iota(jnp.int32, (blk, blk), 0) == lax.broadcasted_iota(jnp.int32, (blk, blk), 1))

    for sb in range(SB):
        for qi in range(LB // blk):
            q = q_ref[sb, qi * blk:(qi + 1) * blk, :]
            kk = kbuf[sb, qi * blk:(qi + 2) * blk, :]
            vv = vbuf[sb, qi * blk:(qi + 2) * blk, :]
            s = lax.dot_general(q, kk, (((1,), (1,)), ((), ())), preferred_element_type=F32)
            s = jnp.where(band_first if qi == 0 else band, s, NEG)
            m = jnp.max(s, axis=1, keepdims=True)
            p = jnp.exp(s - m)
            l = jnp.sum(p, axis=1, keepdims=True)
            o = jnp.dot(p.astype(BF16), vv, preferred_element_type=F32) / l
            o_ref[sb, qi * blk:(qi + 1) * blk, :] = o.astype(o_ref.dtype)
            lse = m + jnp.log(l)
            lse_ref[sb, qi:qi + 1, :] = jnp.sum(
                jnp.where(eye, jnp.broadcast_to(lse, (blk, blk)), 0.0), axis=0, keepdims=True)


def _band_attention(q, k, v, rows_per_step=1024):
    NS, L, Dh = q.shape
    blk = ATT_BLOCK
    LB = min(L, rows_per_step)
    SB = max(1, rows_per_step // LB)
    assert L % LB == 0 and NS % SB == 0 and LB % blk == 0
    nlb = LB // blk
    cur = pl.BlockSpec((SB, LB, Dh), lambda s, n: (s, n, 0))
    prev = pl.BlockSpec((SB, blk, Dh), lambda s, n: (s, jnp.maximum(n * nlb - 1, 0), 0))
    return pl.pallas_call(
        _attn_body,
        grid=(NS // SB, L // LB),
        in_specs=[cur, cur, prev, cur, prev],
        out_specs=[cur, pl.BlockSpec((SB, nlb, blk), lambda s, n: (s, n, 0))],
        out_shape=[jax.ShapeDtypeStruct((NS, L, Dh), BF16),
                   jax.ShapeDtypeStruct((NS, L // blk, blk), F32)],
        scratch_shapes=[pltpu.VMEM((SB, LB + blk, Dh), BF16), pltpu.VMEM((SB, LB + blk, Dh), BF16)],
        compiler_params=_cparams("parallel", "arbitrary"),
        name="band_attention",
    )(q, k, k, v, v)


def _combine_body(*refs, dils):
    G = len(dils)
    o_refs, l_refs = refs[:G], refs[G:2 * G]
    out_ref, acc_ref, wcol_ref = refs[2 * G:]
    S = out_ref.shape[0]
    blk = LANES
    eye = (lax.broadcasted_iota(jnp.int32, (blk, blk), 0) == lax.broadcasted_iota(jnp.int32, (blk, blk), 1))

    lses = [l[...] for l in l_refs]
    mx = functools.reduce(jnp.maximum, lses)
    es = [jnp.exp(l - mx) for l in lses]
    tot = functools.reduce(lambda a, b: a + b, es)
    for g in range(G):
        mix = es[g] / tot
        for tb in range(S // blk):
            seg = jnp.broadcast_to(mix[:, tb * blk:(tb + 1) * blk], (blk, blk))
            col = jnp.sum(jnp.where(eye, seg, 0.0), axis=1, keepdims=True)
            wcol_ref[g * S + tb * blk:g * S + (tb + 1) * blk, :] = jnp.broadcast_to(col, (blk, blk))
    for g, d in enumerate(dils):
        n = S // d
        for r in range(d):
            rows = pl.ds(r, n, stride=d) if d > 1 else pl.ds(0, n)
            wrows = pl.ds(g * S + r, n, stride=d) if d > 1 else pl.ds(g * S, n)
            contrib = o_refs[g][r].astype(F32) * wcol_ref[wrows, :]
            if g == 0:
                acc_ref[rows, :] = contrib
            else:
                acc_ref[rows, :] += contrib
    out_ref[...] = acc_ref[...].astype(out_ref.dtype)


def _combine(os_, lses, B, S, n_heads):
    dils = tuple(d for _, d in ATT_GROUPS)
    in_specs = [pl.BlockSpec((None, None, d, S // d, HEAD_DIM), lambda b, h: (b, h, 0, 0, 0)) for d in dils]
    in_specs += [pl.BlockSpec((None, None, 1, S), lambda b, h: (b, h, 0, 0))] * len(dils)
    return pl.pallas_call(
        functools.partial(_combine_body, dils=dils),
        grid=(B, n_heads),
        in_specs=in_specs,
        out_specs=pl.BlockSpec((None, S, HEAD_DIM), lambda b, h: (b, 0, h)),
        out_shape=jax.ShapeDtypeStruct((B, S, n_heads * HEAD_DIM), BF16),
        scratch_shapes=[pltpu.VMEM((S, HEAD_DIM), F32), pltpu.VMEM((len(dils) * S, LANES), F32)],
        compiler_params=_cparams("parallel", "parallel"),
        name="group_combine",
    )(*os_, *lses)


def kernel(x, mlstm_w_in, mlstm_b_gate, mlstm_norm_g, mlstm_w_out, att_w_q, att_w_kv_shared, att_w_o,
           router_w, router_b, expert_w_gate, expert_w_up, expert_w_down, ln_g, ln_b):
    B, S, D = x.shape
    T = B * S
    depth = ln_g.shape[0]
    n_a = mlstm_w_in.shape[0]
    alpha = (2.0 * depth) ** 0.25

    Hm = mlstm_b_gate.shape[-1] // 2
    dv = mlstm_w_out.shape[1] // Hm
    dk = (mlstm_w_in.shape[-1] - 2 * Hm - 2 * Hm * dv) // (2 * Hm)
    n_main = 2 * Hm * dk + 2 * Hm * dv

    AH = att_w_o.shape[1] // HEAD_DIM
    G = len(ATT_GROUPS)
    AW = G * AH * HEAD_DIM
    E = router_w.shape[1]
    assert E == N_EXPERT_GROUPS * EXPERTS_PER_GROUP

    perm = (jnp.arange(E) % N_EXPERT_GROUPS) * EXPERTS_PER_GROUP + jnp.arange(E) // N_EXPERT_GROUPS
    router = (router_w.T[perm].astype(F32), router_b[perm].reshape(E, 1).astype(F32))

    h = x.reshape(T, D)
    h_b = h.astype(BF16)
    kv = None
    for layer in range(depth):
        if layer < n_a:
            proj = _mm(h_b, mlstm_w_in, layer, 0, n_main, BF16)
            gates_t = _mlstm_gates(h, mlstm_w_in[layer][:, n_main:].T, mlstm_b_gate[layer])
            hm = _mlstm(proj.reshape(B, S, n_main), gates_t, mlstm_norm_g[layer], B, S, Hm, dk, dv)
            mix = _mm(hm.reshape(T, Hm * dv), mlstm_w_out, layer, 0, D, F32)
        else:
            bl = layer - n_a
            k_tabs = _rope_tables(S, 1.0)
            q_tabs = _rope_tables(S, HEAD_DIM ** -0.5)
            w_kv = att_w_kv_shared.reshape(1, D, 2 * AW)
            os_, lses = [], []
            for g, (window, d) in enumerate(ATT_GROUPS):
                assert window // d == ATT_BLOCK and (S // d) % ATT_BLOCK == 0
                if kv is None or len(kv) <= g:
                    kv = (kv or []) + [(
                        _proj_heads(h_b, w_kv, 0, g * AH * HEAD_DIM, AH, d, B, S, k_tabs, True),
                        _proj_heads(h_b, w_kv, 0, AW + g * AH * HEAD_DIM, AH, d, B, S, k_tabs, False))]
                kg, vg = kv[g]
                qg = _proj_heads(h_b, att_w_q, bl, g * AH * HEAD_DIM, AH, d, B, S, q_tabs, True)
                L = S // d
                ns = B * AH * d
                o, lse = _band_attention(qg.reshape(ns, L, HEAD_DIM), kg.reshape(ns, L, HEAD_DIM),
                                         vg.reshape(ns, L, HEAD_DIM))
                os_.append(o.reshape(B, AH, d, L, HEAD_DIM))
                lses.append(lse.reshape(B, AH, d, L).transpose(0, 1, 3, 2).reshape(B, AH, 1, S))
            ob = _combine(os_, lses, B, S, AH)
            mix = _mm(ob.reshape(T, AH * HEAD_DIM), att_w_o, bl, 0, D, F32)

        tm_ln = min(256, T)
        h1_slab, eidx, ew = _ln(T, D, [(h, False, 0), (mix, False, 0)], ln_g[layer, 0], ln_b[layer, 0],
                                alpha, ("slab",), router=router, tm=tm_ln)
        y2 = _moe(h1_slab, eidx, ew, expert_w_gate, expert_w_up, expert_w_down, layer, T, D)
        last = layer == depth - 1
        outs = _ln(T, D, [(h1_slab, True, 0), (y2, True, 0), (y2, True, T // tm_ln)],
                   ln_g[layer, 1], ln_b[layer, 1], alpha, ("f32",) if last else ("f32", "bf16"), tm=tm_ln)
        h = outs[0]
        h_b = None if last else outs[1]
    return h.reshape(B, S, D)
```

```python
import functools
import math

import jax
import jax.numpy as jnp
from jax import lax
from jax.experimental import pallas as pl
from jax.experimental.pallas import tpu as pltpu

F32 = jnp.float32
BF16 = jnp.bfloat16

HEAD_DIM = 128
ATT_GROUPS = ((128, 1), (512, 4), (2048, 16))
ATT_BLOCK = 128
ROPE_DIM = HEAD_DIM // 4
ROPE_THETA = 500000.0
N_EXPERT_GROUPS = 8
EXPERTS_PER_GROUP = 4
NORM_EPS = 1e-5
MLSTM_CHUNK = 256

LANES = 128
SUBLANES = 8
VMEM_LIMIT = 56 * 1024 * 1024
NEG = -1e30


def _cparams(*sem):
    return pltpu.CompilerParams(dimension_semantics=sem, vmem_limit_bytes=VMEM_LIMIT)


def _mm_body(x_ref, w_ref, o_ref, *, w_transposed):
    contract = (((1,), (1 if w_transposed else 0,)), ((), ()))
    o_ref[...] = lax.dot_general(x_ref[...], w_ref[...].astype(BF16), contract,
                                 preferred_element_type=F32).astype(o_ref.dtype)


def _mm(x, w, layer, col0, ncols, out_dtype, w_transposed=False, tm=1024, tn=512):
    M, K = x.shape
    tm = min(tm, M)
    assert M % tm == 0 and ncols % tn == 0 and col0 % tn == 0
    c0 = col0 // tn
    if w_transposed:
        w_spec = pl.BlockSpec((None, tn, K), lambda i, j: (layer, c0 + j, 0))
    else:
        w_spec = pl.BlockSpec((None, K, tn), lambda i, j: (layer, 0, c0 + j))
    return pl.pallas_call(
        functools.partial(_mm_body, w_transposed=w_transposed),
        grid=(M // tm, ncols // tn),
        in_specs=[pl.BlockSpec((tm, K), lambda i, j: (i, 0)), w_spec],
        out_specs=pl.BlockSpec((tm, tn), lambda i, j: (i, j)),
        out_shape=jax.ShapeDtypeStruct((M, ncols), out_dtype),
        compiler_params=_cparams("parallel", "parallel"),
        name="dense_proj",
    )(x, w)


def _gates_body(wt_ref, x_ref, b_ref, o_ref, *, n_heads):
    g = lax.dot_general(wt_ref[...].astype(BF16), x_ref[...], (((1,), (1,)), ((), ())),
                        preferred_element_type=F32)
    g = g + b_ref[...]
    log_sig = jnp.minimum(g, 0.0) - jnp.log1p(jnp.exp(-jnp.abs(g)))
    row = lax.broadcasted_iota(jnp.int32, g.shape, 0)
    o_ref[...] = jnp.where(row >= n_heads, log_sig, g)


def _mlstm_gates(x, w_gates_t, b_gate, tm=512):
    T, D = x.shape
    H2 = w_gates_t.shape[0]
    tm = min(tm, T)
    return pl.pallas_call(
        functools.partial(_gates_body, n_heads=H2 // 2),
        grid=(T // tm,),
        in_specs=[pl.BlockSpec((H2, D), lambda i: (0, 0)),
                  pl.BlockSpec((tm, D), lambda i: (i, 0)),
                  pl.BlockSpec((H2, 1), lambda i: (0, 0))],
        out_specs=pl.BlockSpec((H2, tm), lambda i: (0, i)),
        out_shape=jax.ShapeDtypeStruct((H2, T), F32),
        compiler_params=_cparams("parallel"),
        name="mlstm_gates",
    )(w_gates_t, x, b_gate.reshape(H2, 1).astype(F32))


def _mlstm_body(q_ref, k_ref, v_ref, op_ref, li_ref, lf_ref, g_ref, out_ref,
                c_ref, n_ref, m_ref, *, scale):
    L = q_ref.shape[0]

    @pl.when(pl.program_id(2) == 0)
    def _():
        c_ref[...] = jnp.zeros_like(c_ref)
        n_ref[...] = jnp.zeros_like(n_ref)
        m_ref[...] = jnp.zeros_like(m_ref)

    q = q_ref[...]
    k = k_ref[...]
    v = v_ref[...]
    li = li_ref[...]
    lf = lf_ref[...]
    m_prev = m_ref[...]

    row = lax.broadcasted_iota(jnp.int32, (L, L), 0)
    col = lax.broadcasted_iota(jnp.int32, (L, L), 1)
    causal = col <= row
    eye = col == row

    def to_col(r):
        return jnp.sum(jnp.where(eye, jnp.broadcast_to(r, (L, L)), 0.0), axis=1, keepdims=True)

    b_col = jnp.sum(jnp.where(causal, jnp.broadcast_to(lf, (L, L)), 0.0), axis=1, keepdims=True)
    b_row = jnp.sum(jnp.where(eye, jnp.broadcast_to(b_col, (L, L)), 0.0), axis=0, keepdims=True)
    a_row = li - b_row
    a_b = jnp.broadcast_to(a_row, (L, L))
    mloc = jnp.maximum(m_prev, jnp.max(jnp.where(causal, a_b, NEG), axis=1, keepdims=True))
    dmat = jnp.exp(jnp.where(causal, a_b - mloc, NEG))
    inter = jnp.exp(m_prev - mloc)

    s = lax.dot_general(q, k, (((1,), (1,)), ((), ())), preferred_element_type=F32) * scale * dmat
    qc = jnp.dot(q, c_ref[...].astype(BF16), preferred_element_type=F32) * scale
    num = inter * qc + jnp.dot(s.astype(BF16), v, preferred_element_type=F32)
    qn = jnp.sum(q.astype(F32) * n_ref[...], axis=1, keepdims=True) * scale
    den = inter * qn + jnp.sum(s, axis=1, keepdims=True)
    m_t = b_col + mloc
    h = num / jnp.maximum(jnp.abs(den), jnp.exp(-m_t))

    m_last = jnp.maximum(m_prev, jnp.max(a_row, axis=1, keepdims=True))
    b_last = jnp.sum(lf, axis=1, keepdims=True)
    decay = jnp.exp(m_prev - m_last)
    w_col = to_col(jnp.exp(a_row - m_last))
    wk = w_col * k.astype(F32)
    c_ref[...] = decay * c_ref[...] + lax.dot_general(
        wk.astype(BF16), v, (((0,), (0,)), ((), ())), preferred_element_type=F32)
    n_ref[...] = decay * n_ref[...] + jnp.sum(wk, axis=0, keepdims=True)
    m_ref[...] = b_last + m_last

    hn = h * lax.rsqrt(jnp.mean(h * h, axis=1, keepdims=True) + NORM_EPS)
    gate = 1.0 / (1.0 + jnp.exp(-op_ref[...].astype(F32)))
    out_ref[...] = (hn * g_ref[...] * gate).astype(out_ref.dtype)


def _mlstm(proj, gates_t, norm_g, B, S, H, dk, dv):
    L = min(MLSTM_CHUNK, S)
    nc = S // L
    assert S % L == 0 and (2 * H * dk) % dv == 0
    v0 = (2 * H * dk) // dv
    gates3 = gates_t.reshape(2 * H, 1, B * S)
    return pl.pallas_call(
        functools.partial(_mlstm_body, scale=dk ** -0.5),
        grid=(B, H, nc),
        in_specs=[pl.BlockSpec((None, L, dk), lambda b, h, c: (b, c, h)),
                  pl.BlockSpec((None, L, dk), lambda b, h, c: (b, c, H + h)),
                  pl.BlockSpec((None, L, dv), lambda b, h, c: (b, c, v0 + h)),
                  pl.BlockSpec((None, L, dv), lambda b, h, c: (b, c, v0 + H + h)),
                  pl.BlockSpec((None, 1, L), lambda b, h, c: (h, 0, b * nc + c)),
                  pl.BlockSpec((None, 1, L), lambda b, h, c: (H + h, 0, b * nc + c)),
                  pl.BlockSpec((1, dv), lambda b, h, c: (0, h))],
        out_specs=pl.BlockSpec((None, L, dv), lambda b, h, c: (b, c, h)),
        out_shape=jax.ShapeDtypeStruct((B, S, H * dv), BF16),
        scratch_shapes=[pltpu.VMEM((dk, dv), F32), pltpu.VMEM((1, dk), F32), pltpu.VMEM((1, 1), F32)],
        compiler_params=_cparams("parallel", "parallel", "arbitrary"),
        name="mlstm_chunks",
    )(proj, proj, proj, proj, gates3, gates3, norm_g.reshape(1, H * dv).astype(F32))


def _slab_pitch(D):
    return D // LANES + SUBLANES


def _slab_load(ref, n_tok, D):
    pitch = _slab_pitch(D)
    return jnp.concatenate([ref[pl.ds(c, n_tok, stride=pitch), :] for c in range(D // LANES)], axis=1)


def _slab_store(ref, val, pad=True):
    n_tok, D = val.shape
    pitch = _slab_pitch(D)
    nrow = D // LANES
    for c in range(nrow):
        ref[pl.ds(c, n_tok, stride=pitch), :] = val[:, c * LANES:(c + 1) * LANES].astype(ref.dtype)
    if pad:
        for c in range(nrow, pitch):
            ref[pl.ds(c, n_tok, stride=pitch), :] = jnp.zeros((n_tok, LANES), ref.dtype)


def _ln_body(*refs, tm, slabs, weighted, alpha, router, outs):
    n_in = len(slabs)
    pos = n_in
    if weighted:
        aw = refs[pos][...]
        pos += 1
    g_ref, b_ref = refs[pos], refs[pos + 1]
    pos += 2
    D = g_ref.shape[1]
    vals = [_slab_load(r, tm, D) if sl else r[...].astype(F32) for r, sl in zip(refs[:n_in], slabs)]
    if router:
        wr_ref, br_ref = refs[pos], refs[pos + 1]
        pos += 2

    z = alpha * vals[0]
    for k, a in enumerate(vals[1:]):
        z = z + (a * aw[:, k:k + 1] if weighted else a)
    mu = jnp.mean(z, axis=1, keepdims=True)
    zc = z - mu
    var = jnp.mean(zc * zc, axis=1, keepdims=True)
    y = zc * lax.rsqrt(var + NORM_EPS) * g_ref[...] + b_ref[...]
    for kind in outs:
        if kind == "slab":
            _slab_store(refs[pos], y)
        else:
            refs[pos][...] = y.astype(refs[pos].dtype)
        pos += 1

    if router:
        ei_ref, ew_ref = refs[pos], refs[pos + 1]
        ng, epg = N_EXPERT_GROUPS, EXPERTS_PER_GROUP
        lt = lax.dot_general(wr_ref[...], y, (((1,), (1,)), ((), ())),
                             precision=lax.Precision.HIGHEST, preferred_element_type=F32) + br_ref[...]
        ex = jnp.exp(lt - jnp.max(lt, axis=0, keepdims=True))
        p = ex / jnp.sum(ex, axis=0, keepdims=True)
        p0, p1, p2, p3 = [p[j * ng:(j + 1) * ng] for j in range(epg)]
        a, b = jnp.maximum(p0, p1), jnp.minimum(p0, p1)
        c, d = jnp.maximum(p2, p3), jnp.minimum(p2, p3)
        score = jnp.maximum(a, c) + jnp.maximum(jnp.minimum(a, c), jnp.maximum(b, d))
        gi = lax.broadcasted_iota(jnp.int32, score.shape, 0)
        best = jnp.min(jnp.where(score == jnp.max(score, axis=0, keepdims=True), gi, ng),
                       axis=0, keepdims=True)
        sel = gi == best
        v = [jnp.sum(jnp.where(sel, pj, 0.0), axis=0, keepdims=True) for pj in (p0, p1, p2, p3)]

        def first_max(vals):
            top = jnp.maximum(jnp.maximum(vals[0], vals[1]), jnp.maximum(vals[2], vals[3]))
            idx = jnp.where(vals[0] == top, 0, jnp.where(vals[1] == top, 1, jnp.where(vals[2] == top, 2, 3)))
            return top, idx

        v1, j1 = first_max(v)
        v2, j2 = first_max([jnp.where(j1 == j, -1.0, v[j]) for j in range(epg)])
        tot = v1 + v2
        ei_ref[...] = jnp.concatenate([best * epg + j1, best * epg + j2], axis=0).astype(jnp.int32)
        eye = (lax.broadcasted_iota(jnp.int32, (tm, tm), 0) == lax.broadcasted_iota(jnp.int32, (tm, tm), 1))
        cols = [jnp.sum(jnp.where(eye, jnp.broadcast_to(w, (tm, tm)), 0.0), axis=1, keepdims=True)
                for w in (v1 / tot, v2 / tot)]
        lane = lax.broadcasted_iota(jnp.int32, (tm, LANES), 1)
        ew_ref[...] = jnp.where(lane == 0, cols[0], jnp.where(lane == 1, cols[1], 0.0))


def _ln(T, D, inputs, g, b, alpha, outs, add_w=None, router=None, tm=256):
    tm = min(tm, T)
    pitch = _slab_pitch(D)

    def spec(is_slab, off):
        if is_slab:
            return pl.BlockSpec((tm * pitch, LANES), lambda i: (off + i, 0))
        return pl.BlockSpec((tm, D), lambda i: (off + i, 0))

    in_specs = [spec(sl, off) for _, sl, off in inputs]
    args = [a for a, _, _ in inputs]
    if add_w is not None:
        in_specs.append(pl.BlockSpec((tm, LANES), lambda i: (i, 0)))
        args.append(add_w)
    in_specs += [pl.BlockSpec((1, D), lambda i: (0, 0))] * 2
    args += [g.reshape(1, D).astype(F32), b.reshape(1, D).astype(F32)]
    out_shape, out_specs = [], []
    for kind in outs:
        if kind == "slab":
            out_shape.append(jax.ShapeDtypeStruct((T * pitch, LANES), F32))
        else:
            out_shape.append(jax.ShapeDtypeStruct((T, D), F32 if kind == "f32" else BF16))
        out_specs.append(spec(kind == "slab", 0))
    if router is not None:
        wr_t, br = router
        E = wr_t.shape[0]
        in_specs += [pl.BlockSpec((E, D), lambda i: (0, 0)), pl.BlockSpec((E, 1), lambda i: (0, 0))]
        args += [wr_t, br]
        out_shape += [jax.ShapeDtypeStruct((2, T), jnp.int32), jax.ShapeDtypeStruct((T, LANES), F32)]
        out_specs += [pl.BlockSpec((2, tm), lambda i: (0, i)), pl.BlockSpec((tm, LANES), lambda i: (i, 0))]
    return pl.pallas_call(
        functools.partial(_ln_body, tm=tm, slabs=tuple(sl for _, sl, _ in inputs),
                          weighted=add_w is not None, alpha=alpha,
                          router=router is not None, outs=tuple(outs)),
        grid=(T // tm,),
        in_specs=in_specs, out_specs=out_specs, out_shape=out_shape,
        compiler_params=_cparams("parallel"),
        name="deepnorm_ln_router" if router is not None else "deepnorm_ln",
    )(*args)


FFN_SUB = 256
FFN_VISIT_SUBS = 3


def _ffn_body(ve_ref, vrow_ref, vcnt_ref, nv_ref, ord_ref, h_hbm, wg_ref, wu_ref, wd_ref, y_hbm,
              stage_ref, xb_ref, acc_ref, sem_in, sem_out, *, D, T):
    v, j = pl.program_id(0), pl.program_id(1)
    last_j = pl.num_programs(1) - 1
    pitch = _slab_pitch(D)
    sub = FFN_SUB
    cnt, row0 = vcnt_ref[v], vrow_ref[v]

    def stage_row(r):
        return stage_ref.at[pl.ds(pl.multiple_of(r * pitch, SUBLANES), pitch)]

    @pl.when((v == 0) & (j == 0))
    def _():
        stage_ref[...] = jnp.zeros_like(stage_ref)
        xb_ref[...] = jnp.zeros_like(xb_ref)

    for s in range(FFN_VISIT_SUBS):
        n_rows = jnp.minimum(cnt - s * sub, sub)
        active = (v < nv_ref[0]) & (s * sub < cnt)
        rows = slice(s * sub, (s + 1) * sub)

        @pl.when(active & (j == 0))
        def _():
            def fetch(r, carry):
                a = ord_ref[row0 + s * sub + r]
                tok = jnp.where(a >= T, a - T, a)
                pltpu.make_async_copy(h_hbm.at[tok], stage_row(r), sem_in).start()
                return carry

            def fetched(r, carry):
                pltpu.make_async_copy(h_hbm.at[0], stage_row(r), sem_in).wait()
                return carry

            lax.fori_loop(0, n_rows, fetch, 0)
            lax.fori_loop(0, n_rows, fetched, 0)
            xb_ref[rows, :] = _slab_load(stage_ref, sub, D).astype(BF16)

        @pl.when(active)
        def _():
            x = xb_ref[rows, :]
            g = jnp.dot(x, wg_ref[...].astype(BF16), preferred_element_type=F32)
            u = jnp.dot(x, wu_ref[...].astype(BF16), preferred_element_type=F32)
            hmid = g * (1.0 / (1.0 + jnp.exp(-g))) * u
            part = jnp.dot(hmid.astype(BF16), wd_ref[...].astype(BF16), preferred_element_type=F32)

            @pl.when(j == 0)
            def _():
                acc_ref[rows, :] = part

            @pl.when(j > 0)
            def _():
                acc_ref[rows, :] += part

        @pl.when(active & (j == last_j))
        def _():
            _slab_store(stage_ref, acc_ref[rows, :], pad=False)

            def send(r, carry):
                pltpu.make_async_copy(stage_row(r), y_hbm.at[ord_ref[row0 + s * sub + r]], sem_out).start()
                return carry

            def sent(r, carry):
                pltpu.make_async_copy(stage_row(r), y_hbm.at[0], sem_out).wait()
                return carry

            lax.fori_loop(0, n_rows, send, 0)
            lax.fori_loop(0, n_rows, sent, 0)


def _moe(h_slab, eidx, w_gate, w_up, w_down, layer, T, D):
    E, F = w_gate.shape[1], w_gate.shape[-1]
    A = 2 * T
    pitch = _slab_pitch(D)
    fc = 256 if F % 256 == 0 else F
    visit_rows = FFN_VISIT_SUBS * FFN_SUB
    n_visits = E + A // visit_rows

    e_flat = eidx.reshape(A)
    order = jnp.argsort(e_flat, stable=True).astype(jnp.int32)
    experts = jnp.arange(E, dtype=jnp.int32)
    counts = jnp.sum((e_flat[:, None] == experts[None, :]).astype(jnp.int32), axis=0)
    start = jnp.cumsum(counts) - counts
    visits_e = (counts + visit_rows - 1) // visit_rows
    visit_end = jnp.cumsum(visits_e)
    n_used = visit_end[-1:].astype(jnp.int32)
    vid = jnp.minimum(jnp.arange(n_visits, dtype=jnp.int32), n_used[0] - 1)
    visit_expert = jnp.sum((vid[:, None] >= visit_end[None, :]).astype(jnp.int32), axis=1).astype(jnp.int32)
    k_in_expert = vid - (visit_end - visits_e)[visit_expert]
    visit_row0 = (start[visit_expert] + k_in_expert * visit_rows).astype(jnp.int32)
    visit_cnt = jnp.minimum(counts[visit_expert] - k_in_expert * visit_rows, visit_rows).astype(jnp.int32)

    n_fc = F // fc

    def chunk(v, j, nv):
        return jnp.where(v < nv[0], j, n_fc - 1)

    def w_map(v, j, ve, vrow, vcnt, nv, ordr):
        return (layer, ve[v], 0, chunk(v, j, nv))

    out = pl.pallas_call(
        functools.partial(_ffn_body, D=D, T=T),
        grid_spec=pltpu.PrefetchScalarGridSpec(
            num_scalar_prefetch=5, grid=(n_visits, n_fc),
            in_specs=[pl.BlockSpec(memory_space=pl.ANY),
                      pl.BlockSpec((None, None, D, fc), w_map),
                      pl.BlockSpec((None, None, D, fc), w_map),
                      pl.BlockSpec((None, None, fc, D),
                                   lambda v, j, ve, vrow, vcnt, nv, ordr: (layer, ve[v], chunk(v, j, nv), 0))],
            out_specs=pl.BlockSpec(memory_space=pl.ANY),
            scratch_shapes=[pltpu.VMEM((FFN_SUB * pitch, LANES), F32), pltpu.VMEM((visit_rows, D), BF16),
                            pltpu.VMEM((visit_rows, D), F32), pltpu.SemaphoreType.DMA(()),
                            pltpu.SemaphoreType.DMA(())]),
        out_shape=jax.ShapeDtypeStruct((A, pitch, LANES), F32),
        compiler_params=_cparams("arbitrary", "arbitrary"),
        name="expert_ffn",
    )(visit_expert, visit_row0, visit_cnt, n_used, order, h_slab.reshape(T, pitch, LANES),
      w_gate, w_up, w_down)
    return out.reshape(A * pitch, LANES)


def _proj_heads_body(x_ref, w_ref, ca_ref, sb_ref, sc_ref, o_ref, acc_ref, *, d, hb, rope):
    acc = jnp.dot(x_ref[...], w_ref[...].astype(BF16), preferred_element_type=F32)
    tm = x_ref.shape[0]
    for hh in range(hb):
        acc_ref[hh * tm:(hh + 1) * tm, :] = acc[:, hh * HEAD_DIM:(hh + 1) * HEAD_DIM]
    n = tm // d
    half = ROPE_DIM // 2
    for r in range(d):
        if rope:
            rows = pl.ds(r, n, stride=d) if d > 1 else pl.ds(0, n)
            ca, sb, sc = ca_ref[rows, :], sb_ref[rows, :], sc_ref[rows, :]
        for hh in range(hb):
            val = acc_ref[pl.ds(hh * tm + r, n, stride=d) if d > 1 else pl.ds(hh * tm, n), :]
            if rope:
                val = (val * ca + pltpu.roll(val, half, 1) * sb
                       + pltpu.roll(val, HEAD_DIM - half, 1) * sc)
            o_ref[hh, r] = val.astype(o_ref.dtype)


def _proj_heads(x, w, layer, col0, n_heads, d, B, S, tables, rope, tm=1024, hb=4):
    M, K = x.shape
    tm = min(tm, S)
    tn = hb * HEAD_DIM
    assert S % tm == 0 and n_heads % hb == 0 and col0 % tn == 0 and (tm // d) % 16 == 0
    mt = S // tm
    c0 = col0 // tn
    tab_spec = pl.BlockSpec((tm, HEAD_DIM), lambda i, j: (i % mt, 0))
    return pl.pallas_call(
        functools.partial(_proj_heads_body, d=d, hb=hb, rope=rope),
        grid=(M // tm, n_heads // hb),
        in_specs=[pl.BlockSpec((tm, K), lambda i, j: (i, 0)),
                  pl.BlockSpec((None, K, tn), lambda i, j: (layer, 0, c0 + j)),
                  tab_spec, tab_spec, tab_spec],
        out_specs=pl.BlockSpec((None, hb, d, tm // d, HEAD_DIM), lambda i, j: (i // mt, j, 0, i % mt, 0)),
        out_shape=jax.ShapeDtypeStruct((B, n_heads, d, S // d, HEAD_DIM), BF16),
        scratch_shapes=[pltpu.VMEM((hb * tm, HEAD_DIM), F32)],
        compiler_params=_cparams("parallel", "parallel"),
        name="attn_proj_d%d" % d,
    )(x, w, *tables)


def _rope_tables(S, scale):
    half = ROPE_DIM // 2
    inv_freq = ROPE_THETA ** (-jnp.arange(half, dtype=F32) / half)
    ang = jnp.arange(S).astype(F32)[:, None] * inv_freq[None, :]
    cos, sin = jnp.cos(ang), jnp.sin(ang)
    zeros = jnp.zeros((S, HEAD_DIM - ROPE_DIM), F32)
    ca = jnp.concatenate([cos, cos, jnp.ones_like(zeros)], axis=1)
    sb = jnp.concatenate([jnp.zeros_like(sin), sin, zeros], axis=1)
    sc = jnp.concatenate([-sin, jnp.zeros_like(sin), zeros], axis=1)
    return ca * scale, sb * scale, sc * scale


def _attn_body(q_ref, kc_ref, kp_ref, vc_ref, vp_ref, o_ref, lse_ref, kbuf, vbuf):
    SB, LB, _ = q_ref.shape
    blk = ATT_BLOCK
    n = pl.program_id(1)
    kbuf[:, :blk] = kp_ref[...]
    kbuf[:, blk:] = kc_ref[...]
    vbuf[:, :blk] = vp_ref[...]
    vbuf[:, blk:] = vc_ref[...]

    row = lax.broadcasted_iota(jnp.int32, (blk, 2 * blk), 0)
    col = lax.broadcasted_iota(jnp.int32, (blk, 2 * blk), 1)
    band = (col >= row) & (col <= row + blk)
    band_first = band & ((col >= blk) | (n > 0))
    eye = (lax.broadcasted_iota(jnp.int32, (blk, blk), 0) == lax.broadcasted_iota(jnp.int32, (blk, blk), 1))

    for sb in range(SB):
        for qi in range(LB // blk):
            q = q_ref[sb, qi * blk:(qi + 1) * blk, :]
            kk = kbuf[sb, qi * blk:(qi + 2) * blk, :]
            vv = vbuf[sb, qi * blk:(qi + 2) * blk, :]
            s = lax.dot_general(q, kk, (((1,), (1,)), ((), ())), preferred_element_type=F32)
            s = jnp.where(band_first if qi == 0 else band, s, NEG)
            m = jnp.max(s, axis=1, keepdims=True)
            p = jnp.exp(s - m)
            l = jnp.sum(p, axis=1, keepdims=True)
            o = jnp.dot(p.astype(BF16), vv, preferred_element_type=F32) / l
            o_ref[sb, qi * blk:(qi + 1) * blk, :] = o.astype(o_ref.dtype)
            lse = m + jnp.log(l)
            lse_ref[sb, qi:qi + 1, :] = jnp.sum(
                jnp.where(eye, jnp.broadcast_to(lse, (blk, blk)), 0.0), axis=0, keepdims=True)


def _band_attention(q, k, v, rows_per_step=1024):
    NS, L, Dh = q.shape
    blk = ATT_BLOCK
    LB = min(L, rows_per_step)
    SB = max(1, rows_per_step // LB)
    assert L % LB == 0 and NS % SB == 0 and LB % blk == 0
    nlb = LB // blk
    cur = pl.BlockSpec((SB, LB, Dh), lambda s, n: (s, n, 0))
    prev = pl.BlockSpec((SB, blk, Dh), lambda s, n: (s, jnp.maximum(n * nlb - 1, 0), 0))
    return pl.pallas_call(
        _attn_body,
        grid=(NS // SB, L // LB),
        in_specs=[cur, cur, prev, cur, prev],
        out_specs=[cur, pl.BlockSpec((SB, nlb, blk), lambda s, n: (s, n, 0))],
        out_shape=[jax.ShapeDtypeStruct((NS, L, Dh), BF16),
                   jax.ShapeDtypeStruct((NS, L // blk, blk), F32)],
        scratch_shapes=[pltpu.VMEM((SB, LB + blk, Dh), BF16), pltpu.VMEM((SB, LB + blk, Dh), BF16)],
        compiler_params=_cparams("parallel", "arbitrary"),
        name="band_attention",
    )(q, k, k, v, v)


def _combine_body(*refs, dils):
    G = len(dils)
    o_refs, l_refs = refs[:G], refs[G:2 * G]
    out_ref, acc_ref, wcol_ref = refs[2 * G:]
    S = out_ref.shape[0]
    blk = LANES
    eye = (lax.broadcasted_iota(jnp.int32, (blk, blk), 0) == lax.broadcasted_iota(jnp.int32, (blk, blk), 1))

    lses = [l[...] for l in l_refs]
    mx = functools.reduce(jnp.maximum, lses)
    es = [jnp.exp(l - mx) for l in lses]
    tot = functools.reduce(lambda a, b: a + b, es)
    for g in range(G):
        mix = es[g] / tot
        for tb in range(S // blk):
            seg = jnp.broadcast_to(mix[:, tb * blk:(tb + 1) * blk], (blk, blk))
            col = jnp.sum(jnp.where(eye, seg, 0.0), axis=1, keepdims=True)
            wcol_ref[g * S + tb * blk:g * S + (tb + 1) * blk, :] = jnp.broadcast_to(col, (blk, blk))
    for g, d in enumerate(dils):
        n = S // d
        for r in range(d):
            rows = pl.ds(r, n, stride=d) if d > 1 else pl.ds(0, n)
            wrows = pl.ds(g * S + r, n, stride=d) if d > 1 else pl.ds(g * S, n)
            contrib = o_refs[g][r].astype(F32) * wcol_ref[wrows, :]
            if g == 0:
                acc_ref[rows, :] = contrib
            else:
                acc_ref[rows, :] += contrib
    out_ref[...] = acc_ref[...].astype(out_ref.dtype)


def _combine(os_, lses, B, S, n_heads):
    dils = tuple(d for _, d in ATT_GROUPS)
    in_specs = [pl.BlockSpec((None, None, d, S // d, HEAD_DIM), lambda b, h: (b, h, 0, 0, 0)) for d in dils]
    in_specs += [pl.BlockSpec((None, None, 1, S), lambda b, h: (b, h, 0, 0))] * len(dils)
    return pl.pallas_call(
        functools.partial(_combine_body, dils=dils),
        grid=(B, n_heads),
        in_specs=in_specs,
        out_specs=pl.BlockSpec((None, S, HEAD_DIM), lambda b, h: (b, 0, h)),
        out_shape=jax.ShapeDtypeStruct((B, S, n_heads * HEAD_DIM), BF16),
        scratch_shapes=[pltpu.VMEM((S, HEAD_DIM), F32), pltpu.VMEM((len(dils) * S, LANES), F32)],
        compiler_params=_cparams("parallel", "parallel"),
        name="group_combine",
    )(*os_, *lses)


def kernel(x, mlstm_w_in, mlstm_b_gate, mlstm_norm_g, mlstm_w_out, att_w_q, att_w_kv_shared, att_w_o,
           router_w, router_b, expert_w_gate, expert_w_up, expert_w_down, ln_g, ln_b):
    B, S, D = x.shape
    T = B * S
    depth = ln_g.shape[0]
    n_a = mlstm_w_in.shape[0]
    alpha = (2.0 * depth) ** 0.25

    Hm = mlstm_b_gate.shape[-1] // 2
    dv = mlstm_w_out.shape[1] // Hm
    dk = (mlstm_w_in.shape[-1] - 2 * Hm - 2 * Hm * dv) // (2 * Hm)
    n_main = 2 * Hm * dk + 2 * Hm * dv

    AH = att_w_o.shape[1] // HEAD_DIM
    G = len(ATT_GROUPS)
    AW = G * AH * HEAD_DIM
    E = router_w.shape[1]
    assert E == N_EXPERT_GROUPS * EXPERTS_PER_GROUP

    perm = (jnp.arange(E) % N_EXPERT_GROUPS) * EXPERTS_PER_GROUP + jnp.arange(E) // N_EXPERT_GROUPS
    router = (router_w.T[perm].astype(F32), router_b[perm].reshape(E, 1).astype(F32))

    h = x.reshape(T, D)
    h_b = h.astype(BF16)
    kv = None
    for layer in range(depth):
        if layer < n_a:
            w_in_t = jnp.swapaxes(mlstm_w_in, 1, 2)
            proj = _mm(h_b, w_in_t, layer, 0, n_main, BF16, w_transposed=True)
            gates_t = _mlstm_gates(h_b, w_in_t[layer, n_main:], mlstm_b_gate[layer])
            hm = _mlstm(proj.reshape(B, S, n_main), gates_t, mlstm_norm_g[layer], B, S, Hm, dk, dv)
            mix = _mm(hm.reshape(T, Hm * dv), mlstm_w_out, layer, 0, D, BF16)
        else:
            bl = layer - n_a
            k_tabs = _rope_tables(S, 1.0)
            q_tabs = _rope_tables(S, HEAD_DIM ** -0.5)
            w_kv = att_w_kv_shared.reshape(1, D, 2 * AW)
            os_, lses = [], []
            for g, (window, d) in enumerate(ATT_GROUPS):
                assert window // d == ATT_BLOCK and (S // d) % ATT_BLOCK == 0
                if kv is None or len(kv) <= g:
                    kv = (kv or []) + [(
                        _proj_heads(h_b, w_kv, 0, g * AH * HEAD_DIM, AH, d, B, S, k_tabs, True),
                        _proj_heads(h_b, w_kv, 0, AW + g * AH * HEAD_DIM, AH, d, B, S, k_tabs, False))]
                kg, vg = kv[g]
                qg = _proj_heads(h_b, att_w_q, bl, g * AH * HEAD_DIM, AH, d, B, S, q_tabs, True)
                L = S // d
                ns = B * AH * d
                o, lse = _band_attention(qg.reshape(ns, L, HEAD_DIM), kg.reshape(ns, L, HEAD_DIM),
                                         vg.reshape(ns, L, HEAD_DIM))
                os_.append(o.reshape(B, AH, d, L, HEAD_DIM))
                lses.append(lse.reshape(B, AH, d, L).transpose(0, 1, 3, 2).reshape(B, AH, 1, S))
            ob = _combine(os_, lses, B, S, AH)
            mix = _mm(ob.reshape(T, AH * HEAD_DIM), att_w_o, bl, 0, D, BF16)

        tm_ln = min(256, T)
        h1_slab, eidx, ew = _ln(T, D, [(h, False, 0), (mix, False, 0)], ln_g[layer, 0], ln_b[layer, 0],
                                alpha, ("slab",), router=router, tm=tm_ln)
        y2 = _moe(h1_slab, eidx, expert_w_gate, expert_w_up, expert_w_down, layer, T, D)
        last = layer == depth - 1
        outs = _ln(T, D, [(h1_slab, True, 0), (y2, True, 0), (y2, True, T // tm_ln)],
                   ln_g[layer, 1], ln_b[layer, 1], alpha, ("f32",) if last else ("f32", "bf16"),
                   add_w=ew, tm=tm_ln)
        h = outs[0]
        h_b = None if last else outs[1]
    return h.reshape(B, S, D)
```

```python
import functools
import math

import jax
import jax.numpy as jnp
from jax import lax
from jax.experimental import pallas as pl
from jax.experimental.pallas import tpu as pltpu

F32 = jnp.float32
BF16 = jnp.bfloat16

HEAD_DIM = 128
ATT_GROUPS = ((128, 1), (512, 4), (2048, 16))
ATT_BLOCK = 128
ROPE_DIM = HEAD_DIM // 4
ROPE_THETA = 500000.0
N_EXPERT_GROUPS = 8
EXPERTS_PER_GROUP = 4
NORM_EPS = 1e-5
MLSTM_CHUNK = 256
LANES = 128
SUBLANES = 8
VMEM_LIMIT = 56 * 1024 * 1024
NEG = -1e30


def _cparams(*sem):
    return pltpu.CompilerParams(dimension_semantics=sem, vmem_limit_bytes=VMEM_LIMIT)


def _mm_body(x_ref, w_ref, o_ref, *, w_transposed):
    contract = (((1,), (1 if w_transposed else 0,)), ((), ()))
    o_ref[...] = lax.dot_general(x_ref[...], w_ref[...].astype(BF16), contract,
                                 preferred_element_type=F32).astype(o_ref.dtype)


def _mm(x, w, layer, col0, ncols, out_dtype, w_transposed=False, tm=1024, tn=512):
    M, K = x.shape
    tm = min(tm, M)
    assert M % tm == 0 and ncols % tn == 0 and col0 % tn == 0
    c0 = col0 // tn
    if w_transposed:
        w_spec = pl.BlockSpec((None, tn, K), lambda i, j: (layer, c0 + j, 0))
    else:
        w_spec = pl.BlockSpec((None, K, tn), lambda i, j: (layer, 0, c0 + j))
    return pl.pallas_call(
        functools.partial(_mm_body, w_transposed=w_transposed),
        grid=(M // tm, ncols // tn),
        in_specs=[pl.BlockSpec((tm, K), lambda i, j: (i, 0)), w_spec],
        out_specs=pl.BlockSpec((tm, tn), lambda i, j: (i, j)),
        out_shape=jax.ShapeDtypeStruct((M, ncols), out_dtype),
        compiler_params=_cparams("parallel", "parallel"),
        name="dense_proj",
    )(x, w)


def _gates_body(wt_ref, x_ref, b_ref, o_ref, *, n_heads):
    g = lax.dot_general(wt_ref[...].astype(BF16), x_ref[...], (((1,), (1,)), ((), ())),
                        preferred_element_type=F32)
    g = g + b_ref[...]
    log_sig = jnp.minimum(g, 0.0) - jnp.log1p(jnp.exp(-jnp.abs(g)))
    row = lax.broadcasted_iota(jnp.int32, g.shape, 0)
    o_ref[...] = jnp.where(row >= n_heads, log_sig, g)


def _mlstm_gates(x, w_gates_t, b_gate, tm=512):
    T, D = x.shape
    H2 = w_gates_t.shape[0]
    tm = min(tm, T)
    return pl.pallas_call(
        functools.partial(_gates_body, n_heads=H2 // 2),
        grid=(T // tm,),
        in_specs=[pl.BlockSpec((H2, D), lambda i: (0, 0)),
                  pl.BlockSpec((tm, D), lambda i: (i, 0)),
                  pl.BlockSpec((H2, 1), lambda i: (0, 0))],
        out_specs=pl.BlockSpec((H2, tm), lambda i: (0, i)),
        out_shape=jax.ShapeDtypeStruct((H2, T), F32),
        compiler_params=_cparams("parallel"),
        name="mlstm_gates",
    )(w_gates_t, x, b_gate.reshape(H2, 1).astype(F32))


def _mlstm_body(q_ref, k_ref, v_ref, op_ref, li_ref, lf_ref, g_ref, out_ref,
                c_ref, n_ref, m_ref, *, scale):
    L = q_ref.shape[0]

    @pl.when(pl.program_id(2) == 0)
    def _():
        c_ref[...] = jnp.zeros_like(c_ref)
        n_ref[...] = jnp.zeros_like(n_ref)
        m_ref[...] = jnp.zeros_like(m_ref)

    q = q_ref[...]
    k = k_ref[...]
    v = v_ref[...]
    li = li_ref[...]
    lf = lf_ref[...]
    m_prev = m_ref[...]

    row = lax.broadcasted_iota(jnp.int32, (L, L), 0)
    col = lax.broadcasted_iota(jnp.int32, (L, L), 1)
    causal = col <= row
    eye = col == row

    def to_col(r):
        return jnp.sum(jnp.where(eye, jnp.broadcast_to(r, (L, L)), 0.0), axis=1, keepdims=True)

    b_col = jnp.sum(jnp.where(causal, jnp.broadcast_to(lf, (L, L)), 0.0), axis=1, keepdims=True)
    b_row = jnp.sum(jnp.where(eye, jnp.broadcast_to(b_col, (L, L)), 0.0), axis=0, keepdims=True)
    a_row = li - b_row
    a_b = jnp.broadcast_to(a_row, (L, L))
    mloc = jnp.maximum(m_prev, jnp.max(jnp.where(causal, a_b, NEG), axis=1, keepdims=True))
    dmat = jnp.exp(jnp.where(causal, a_b - mloc, NEG))
    inter = jnp.exp(m_prev - mloc)

    s = lax.dot_general(q, k, (((1,), (1,)), ((), ())), preferred_element_type=F32) * scale * dmat
    qc = jnp.dot(q, c_ref[...].astype(BF16), preferred_element_type=F32) * scale
    num = inter * qc + jnp.dot(s.astype(BF16), v, preferred_element_type=F32)
    qn = jnp.sum(q.astype(F32) * n_ref[...], axis=1, keepdims=True) * scale
    den = inter * qn + jnp.sum(s, axis=1, keepdims=True)
    m_t = b_col + mloc
    h = num / jnp.maximum(jnp.abs(den), jnp.exp(-m_t))

    m_last = jnp.maximum(m_prev, jnp.max(a_row, axis=1, keepdims=True))
    b_last = jnp.sum(lf, axis=1, keepdims=True)
    decay = jnp.exp(m_prev - m_last)
    w_col = to_col(jnp.exp(a_row - m_last))
    wk = w_col * k.astype(F32)
    c_ref[...] = decay * c_ref[...] + lax.dot_general(
        wk.astype(BF16), v, (((0,), (0,)), ((), ())), preferred_element_type=F32)
    n_ref[...] = decay * n_ref[...] + jnp.sum(wk, axis=0, keepdims=True)
    m_ref[...] = b_last + m_last

    hn = h * lax.rsqrt(jnp.mean(h * h, axis=1, keepdims=True) + NORM_EPS)
    gate = 1.0 / (1.0 + jnp.exp(-op_ref[...].astype(F32)))
    out_ref[...] = (hn * g_ref[...] * gate).astype(out_ref.dtype)


def _mlstm(proj, gates_t, norm_g, B, S, H, dk, dv):
    L = min(MLSTM_CHUNK, S)
    nc = S // L
    assert S % L == 0 and (2 * H * dk) % dv == 0
    v0 = (2 * H * dk) // dv
    gates3 = gates_t.reshape(2 * H, 1, B * S)
    return pl.pallas_call(
        functools.partial(_mlstm_body, scale=dk ** -0.5),
        grid=(B, H, nc),
        in_specs=[pl.BlockSpec((None, L, dk), lambda b, h, c: (b, c, h)),
                  pl.BlockSpec((None, L, dk), lambda b, h, c: (b, c, H + h)),
                  pl.BlockSpec((None, L, dv), lambda b, h, c: (b, c, v0 + h)),
                  pl.BlockSpec((None, L, dv), lambda b, h, c: (b, c, v0 + H + h)),
                  pl.BlockSpec((None, 1, L), lambda b, h, c: (h, 0, b * nc + c)),
                  pl.BlockSpec((None, 1, L), lambda b, h, c: (H + h, 0, b * nc + c)),
                  pl.BlockSpec((1, dv), lambda b, h, c: (0, h))],
        out_specs=pl.BlockSpec((None, L, dv), lambda b, h, c: (b, c, h)),
        out_shape=jax.ShapeDtypeStruct((B, S, H * dv), BF16),
        scratch_shapes=[pltpu.VMEM((dk, dv), F32), pltpu.VMEM((1, dk), F32), pltpu.VMEM((1, 1), F32)],
        compiler_params=_cparams("parallel", "parallel", "arbitrary"),
        name="mlstm_chunks",
    )(proj, proj, proj, proj, gates3, gates3, norm_g.reshape(1, H * dv).astype(F32))


def _slab_pitch(D):
    return D // LANES + SUBLANES


def _slab_load(ref, n_tok, D):
    pitch = _slab_pitch(D)
    return jnp.concatenate([ref[pl.ds(c, n_tok, stride=pitch), :] for c in range(D // LANES)], axis=1)


def _slab_store(ref, val, pad=True):
    n_tok, D = val.shape
    pitch = _slab_pitch(D)
    nrow = D // LANES
    for c in range(nrow):
        ref[pl.ds(c, n_tok, stride=pitch), :] = val[:, c * LANES:(c + 1) * LANES].astype(ref.dtype)
    if pad:
        for c in range(nrow, pitch):
            ref[pl.ds(c, n_tok, stride=pitch), :] = jnp.zeros((n_tok, LANES), ref.dtype)


def _ln_body(*refs, tm, slabs, weighted, alpha, router, outs):
    n_in = len(slabs)
    pos = n_in
    if weighted:
        aw = refs[pos][...]
        pos += 1
    g_ref, b_ref = refs[pos], refs[pos + 1]
    pos += 2
    D = g_ref.shape[1]
    vals = [_slab_load(r, tm, D) if sl else r[...].astype(F32) for r, sl in zip(refs[:n_in], slabs)]
    if router:
        wr_ref, br_ref = refs[pos], refs[pos + 1]
        pos += 2

    z = alpha * vals[0]
    for k, a in enumerate(vals[1:]):
        z = z + (a * aw[:, k:k + 1] if weighted else a)
    mu = jnp.mean(z, axis=1, keepdims=True)
    zc = z - mu
    var = jnp.mean(zc * zc, axis=1, keepdims=True)
    y = zc * lax.rsqrt(var + NORM_EPS) * g_ref[...] + b_ref[...]
    for kind in outs:
        if kind == "slab":
            _slab_store(refs[pos], y)
        else:
            refs[pos][...] = y.astype(refs[pos].dtype)
        pos += 1

    if router:
        ei_ref, ew_ref = refs[pos], refs[pos + 1]
        ng, epg = N_EXPERT_GROUPS, EXPERTS_PER_GROUP
        lt = lax.dot_general(wr_ref[...], y, (((1,), (1,)), ((), ())),
                             precision=lax.Precision.HIGHEST, preferred_element_type=F32) + br_ref[...]
        ex = jnp.exp(lt - jnp.max(lt, axis=0, keepdims=True))
        p = ex / jnp.sum(ex, axis=0, keepdims=True)
        p0, p1, p2, p3 = [p[j * ng:(j + 1) * ng] for j in range(epg)]
        a, b = jnp.maximum(p0, p1), jnp.minimum(p0, p1)
        c, d = jnp.maximum(p2, p3), jnp.minimum(p2, p3)
        score = jnp.maximum(a, c) + jnp.maximum(jnp.minimum(a, c), jnp.maximum(b, d))
        gi = lax.broadcasted_iota(jnp.int32, score.shape, 0)
        best = jnp.min(jnp.where(score == jnp.max(score, axis=0, keepdims=True), gi, ng),
                       axis=0, keepdims=True)
        sel = gi == best
        v = [jnp.sum(jnp.where(sel, pj, 0.0), axis=0, keepdims=True) for pj in (p0, p1, p2, p3)]

        def first_max(vals):
            top = jnp.maximum(jnp.maximum(vals[0], vals[1]), jnp.maximum(vals[2], vals[3]))
            idx = jnp.where(vals[0] == top, 0, jnp.where(vals[1] == top, 1, jnp.where(vals[2] == top, 2, 3)))
            return top, idx

        v1, j1 = first_max(v)
        v2, j2 = first_max([jnp.where(j1 == j, -1.0, v[j]) for j in range(epg)])
        tot = v1 + v2
        ei_ref[...] = jnp.concatenate([best * epg + j1, best * epg + j2], axis=0).astype(jnp.int32)
        eye = (lax.broadcasted_iota(jnp.int32, (tm, tm), 0) == lax.broadcasted_iota(jnp.int32, (tm, tm), 1))
        cols = [jnp.sum(jnp.where(eye, jnp.broadcast_to(w, (tm, tm)), 0.0), axis=1, keepdims=True)
                for w in (v1 / tot, v2 / tot)]
        lane = lax.broadcasted_iota(jnp.int32, (tm, LANES), 1)
        ew_ref[...] = jnp.where(lane == 0, cols[0], jnp.where(lane == 1, cols[1], 0.0))


def _ln(T, D, inputs, g, b, alpha, outs, add_w=None, router=None, tm=256):
    tm = min(tm, T)
    pitch = _slab_pitch(D)

    def spec(is_slab, off):
        if is_slab:
            return pl.BlockSpec((tm * pitch, LANES), lambda i: (off + i, 0))
        return pl.BlockSpec((tm, D), lambda i: (off + i, 0))

    in_specs = [spec(sl, off) for _, sl, off in inputs]
    args = [a for a, _, _ in inputs]
    if add_w is not None:
        in_specs.append(pl.BlockSpec((tm, LANES), lambda i: (i, 0)))
        args.append(add_w)
    in_specs += [pl.BlockSpec((1, D), lambda i: (0, 0))] * 2
    args += [g.reshape(1, D).astype(F32), b.reshape(1, D).astype(F32)]
    out_shape, out_specs = [], []
    for kind in outs:
        if kind == "slab":
            out_shape.append(jax.ShapeDtypeStruct((T * pitch, LANES), F32))
        else:
            out_shape.append(jax.ShapeDtypeStruct((T, D), F32 if kind == "f32" else BF16))
        out_specs.append(spec(kind == "slab", 0))
    if router is not None:
        wr_t, br = router
        E = wr_t.shape[0]
        in_specs += [pl.BlockSpec((E, D), lambda i: (0, 0)), pl.BlockSpec((E, 1), lambda i: (0, 0))]
        args += [wr_t, br]
        out_shape += [jax.ShapeDtypeStruct((2, T), jnp.int32), jax.ShapeDtypeStruct((T, LANES), F32)]
        out_specs += [pl.BlockSpec((2, tm), lambda i: (0, i)), pl.BlockSpec((tm, LANES), lambda i: (i, 0))]
    return pl.pallas_call(
        functools.partial(_ln_body, tm=tm, slabs=tuple(sl for _, sl, _ in inputs),
                          weighted=add_w is not None, alpha=alpha,
                          router=router is not None, outs=tuple(outs)),
        grid=(T // tm,),
        in_specs=in_specs, out_specs=out_specs, out_shape=out_shape,
        compiler_params=_cparams("parallel"),
        name="deepnorm_ln_router" if router is not None else "deepnorm_ln",
    )(*args)


FFN_SUB = 256
FFN_VISIT_SUBS = 3


def _ffn_body(ve_ref, vrow_ref, vcnt_ref, nv_ref, ord_ref, h_hbm, wg_ref, wu_ref, wd_ref, y_hbm,
              gin_ref, gout_ref, xb_ref, acc_ref, pend_ref, sem_in, sem_out, *, D, T):
    v, j = pl.program_id(0), pl.program_id(1)
    n_fc = pl.num_programs(1)
    last_j = n_fc - 1
    pitch = _slab_pitch(D)
    n_data = D // LANES
    sub = FFN_SUB
    used = v < nv_ref[0]
    cnt, row0 = vcnt_ref[v], vrow_ref[v]

    def slab_row(ref, r):
        return ref.at[pl.ds(pl.multiple_of(r * pitch, SUBLANES), n_data)]

    def start_fetch(first_row, n):
        def body(r, carry):
            a = ord_ref[first_row + r]
            tok = jnp.where(a >= T, a - T, a)
            pltpu.make_async_copy(h_hbm.at[tok, pl.ds(0, n_data)], slab_row(gin_ref, r), sem_in).start()
            return carry
        lax.fori_loop(0, n, body, 0)

    def wait_fetch(n):
        def body(r, carry):
            pltpu.make_async_copy(h_hbm.at[0, pl.ds(0, n_data)], slab_row(gin_ref, r), sem_in).wait()
            return carry
        lax.fori_loop(0, n, body, 0)

    def out_row(r):
        return gout_ref.at[pl.ds(pl.multiple_of(r * pitch, SUBLANES), pitch)]

    def start_send(first_row, n):
        def body(r, carry):
            pltpu.make_async_copy(out_row(r), y_hbm.at[ord_ref[first_row + r]], sem_out).start()
            return carry
        lax.fori_loop(0, n, body, 0)

    def wait_sends():
        def body(r, carry):
            pltpu.make_async_copy(out_row(r), y_hbm.at[0], sem_out).wait()
            return carry
        lax.fori_loop(0, pend_ref[0], body, 0)
        pend_ref[0] = 0

    @pl.when((v == 0) & (j == 0))
    def _():
        gin_ref[...] = jnp.zeros_like(gin_ref)
        gout_ref[...] = jnp.zeros_like(gout_ref)
        xb_ref[...] = jnp.zeros_like(xb_ref)
        pend_ref[0] = 0
        start_fetch(row0, jnp.minimum(cnt, sub))

    for s in range(FFN_VISIT_SUBS):
        n_rows = jnp.minimum(cnt - s * sub, sub)
        active = used & (s * sub < cnt)
        rows = slice(s * sub, (s + 1) * sub)

        @pl.when(active & (j == 0))
        def _():
            wait_fetch(n_rows)
            xb_ref[rows, :] = _slab_load(gin_ref, sub, D).astype(BF16)
            if s + 1 < FFN_VISIT_SUBS:
                @pl.when((s + 1) * sub < cnt)
                def _():
                    start_fetch(row0 + (s + 1) * sub, jnp.minimum(cnt - (s + 1) * sub, sub))

        @pl.when(active)
        def _():
            x = xb_ref[rows, :]
            g = jnp.dot(x, wg_ref[...].astype(BF16), preferred_element_type=F32)
            u = jnp.dot(x, wu_ref[...].astype(BF16), preferred_element_type=F32)
            hmid = g * (1.0 / (1.0 + jnp.exp(-g))) * u
            part = jnp.dot(hmid.astype(BF16), wd_ref[...].astype(BF16), preferred_element_type=F32)

            @pl.when(j == 0)
            def _():
                acc_ref[rows, :] = part

            @pl.when(j > 0)
            def _():
                acc_ref[rows, :] += part

        @pl.when(active & (j == last_j))
        def _():
            wait_sends()
            _slab_store(gout_ref, acc_ref[rows, :], pad=False)
            start_send(row0 + s * sub, n_rows)
            pend_ref[0] = n_rows

    @pl.when((j == jnp.minimum(1, last_j)) & (v + 1 < nv_ref[0]))
    def _():
        start_fetch(vrow_ref[v + 1], jnp.minimum(vcnt_ref[v + 1], sub))

    @pl.when((v == pl.num_programs(0) - 1) & (j == last_j))
    def _():
        wait_sends()


def _moe(h_slab, eidx, w_gate, w_up, w_down, layer, T, D):
    E, F = w_gate.shape[1], w_gate.shape[-1]
    A = 2 * T
    pitch = _slab_pitch(D)
    fc = 256 if F % 256 == 0 else F
    visit_rows = FFN_VISIT_SUBS * FFN_SUB
    n_visits = E + A // visit_rows

    e_flat = eidx.reshape(A)
    order = jnp.argsort(e_flat, stable=True).astype(jnp.int32)
    experts = jnp.arange(E, dtype=jnp.int32)
    counts = jnp.sum((e_flat[:, None] == experts[None, :]).astype(jnp.int32), axis=0)
    start = jnp.cumsum(counts) - counts
    visits_e = (counts + visit_rows - 1) // visit_rows
    visit_end = jnp.cumsum(visits_e)
    n_used = visit_end[-1:].astype(jnp.int32)
    vid = jnp.minimum(jnp.arange(n_visits, dtype=jnp.int32), n_used[0] - 1)
    visit_expert = jnp.sum((vid[:, None] >= visit_end[None, :]).astype(jnp.int32), axis=1).astype(jnp.int32)
    k_in_expert = vid - (visit_end - visits_e)[visit_expert]
    visit_row0 = (start[visit_expert] + k_in_expert * visit_rows).astype(jnp.int32)
    visit_cnt = jnp.minimum(counts[visit_expert] - k_in_expert * visit_rows, visit_rows).astype(jnp.int32)

    n_fc = F // fc

    def chunk(v, j, nv):
        return jnp.where(v < nv[0], j, n_fc - 1)

    def w_map(v, j, ve, vrow, vcnt, nv, ordr):
        return (layer, ve[v], 0, chunk(v, j, nv))

    out = pl.pallas_call(
        functools.partial(_ffn_body, D=D, T=T),
        grid_spec=pltpu.PrefetchScalarGridSpec(
            num_scalar_prefetch=5, grid=(n_visits, n_fc),
            in_specs=[pl.BlockSpec(memory_space=pl.ANY),
                      pl.BlockSpec((None, None, D, fc), w_map),
                      pl.BlockSpec((None, None, D, fc), w_map),
                      pl.BlockSpec((None, None, fc, D),
                                   lambda v, j, ve, vrow, vcnt, nv, ordr: (layer, ve[v], chunk(v, j, nv), 0))],
            out_specs=pl.BlockSpec(memory_space=pl.ANY),
            scratch_shapes=[pltpu.VMEM((FFN_SUB * pitch, LANES), F32), pltpu.VMEM((FFN_SUB * pitch, LANES), F32),
                            pltpu.VMEM((visit_rows, D), BF16), pltpu.VMEM((visit_rows, D), F32),
                            pltpu.SMEM((1,), jnp.int32), pltpu.SemaphoreType.DMA(()),
                            pltpu.SemaphoreType.DMA(())]),
        out_shape=jax.ShapeDtypeStruct((A, pitch, LANES), F32),
        compiler_params=_cparams("arbitrary", "arbitrary"),
        name="expert_ffn",
    )(visit_expert, visit_row0, visit_cnt, n_used, order, h_slab.reshape(T, pitch, LANES),
      w_gate, w_up, w_down)
    return out.reshape(A * pitch, LANES)


def _proj_heads_body(x_ref, w_ref, ca_ref, sb_ref, sc_ref, o_ref, acc_ref, *, d, hb, rope):
    acc = jnp.dot(x_ref[...], w_ref[...].astype(BF16), preferred_element_type=F32)
    tm = x_ref.shape[0]
    n = tm // d
    half = ROPE_DIM // 2
    if d > 1:
        for hh in range(hb):
            acc_ref[hh * tm:(hh + 1) * tm, :] = acc[:, hh * HEAD_DIM:(hh + 1) * HEAD_DIM]
    for r in range(d):
        if rope:
            rows = pl.ds(r, n, stride=d) if d > 1 else pl.ds(0, n)
            ca, sb, sc = ca_ref[rows, :], sb_ref[rows, :], sc_ref[rows, :]
        for hh in range(hb):
            if d > 1:
                val = acc_ref[pl.ds(hh * tm + r, n, stride=d), :]
            else:
                val = acc[:, hh * HEAD_DIM:(hh + 1) * HEAD_DIM]
            if rope:
                val = (val * ca + pltpu.roll(val, half, 1) * sb
                       + pltpu.roll(val, HEAD_DIM - half, 1) * sc)
            o_ref[hh, r] = val.astype(o_ref.dtype)


def _proj_heads(x, w, layer, col0, n_heads, d, B, S, tables, rope, tm=1024, hb=4):
    M, K = x.shape
    tm = min(tm, S)
    tn = hb * HEAD_DIM
    assert S % tm == 0 and n_heads % hb == 0 and col0 % tn == 0 and (tm // d) % 16 == 0
    mt = S // tm
    c0 = col0 // tn
    tab_spec = pl.BlockSpec((tm, HEAD_DIM), lambda i, j: (i % mt, 0))
    return pl.pallas_call(
        functools.partial(_proj_heads_body, d=d, hb=hb, rope=rope),
        grid=(M // tm, n_heads // hb),
        in_specs=[pl.BlockSpec((tm, K), lambda i, j: (i, 0)),
                  pl.BlockSpec((None, K, tn), lambda i, j: (layer, 0, c0 + j)),
                  tab_spec, tab_spec, tab_spec],
        out_specs=pl.BlockSpec((None, hb, d, tm // d, HEAD_DIM), lambda i, j: (i // mt, j, 0, i % mt, 0)),
        out_shape=jax.ShapeDtypeStruct((B, n_heads, d, S // d, HEAD_DIM), BF16),
        scratch_shapes=[pltpu.VMEM((hb * tm, HEAD_DIM), F32)],
        compiler_params=_cparams("parallel", "parallel"),
        name="attn_proj_d%d" % d,
    )(x, w, *tables)


def _rope_tables(S, scale):
    half = ROPE_DIM // 2
    inv_freq = ROPE_THETA ** (-jnp.arange(half, dtype=F32) / half)
    ang = jnp.arange(S).astype(F32)[:, None] * inv_freq[None, :]
    cos, sin = jnp.cos(ang), jnp.sin(ang)
    zeros = jnp.zeros((S, HEAD_DIM - ROPE_DIM), F32)
    ca = jnp.concatenate([cos, cos, jnp.ones_like(zeros)], axis=1)
    sb = jnp.concatenate([jnp.zeros_like(sin), sin, zeros], axis=1)
    sc = jnp.concatenate([-sin, jnp.zeros_like(sin), zeros], axis=1)
    return ca * scale, sb * scale, sc * scale


def _attn_body(q_ref, kc_ref, kp_ref, vc_ref, vp_ref, o_ref, lse_ref, kbuf, vbuf):
    SB, LB, _ = q_ref.shape
    blk = ATT_BLOCK
    n = pl.program_id(1)
    kbuf[:, :blk] = kp_ref[...]
    kbuf[:, blk:] = kc_ref[...]
    vbuf[:, :blk] = vp_ref[...]
    vbuf[:, blk:] = vc_ref[...]

    row = lax.broadcasted_iota(jnp.int32, (blk, 2 * blk), 0)
    col = lax.broadcasted_iota(jnp.int32, (blk, 2 * blk), 1)
    band = (col >= row) & (col <= row + blk)
    band_first = band & ((col >= blk) | (n > 0))
    eye = (lax.broadcasted_iota(jnp.int32, (blk, blk), 0) == lax.broadcasted_iota(jnp.int32, (blk, blk), 1))

    for sb in range(SB):
        for qi in range(LB // blk):
            q = q_ref[sb, qi * blk:(qi + 1) * blk, :]
            kk = kbuf[sb, qi * blk:(qi + 2) * blk, :]
            vv = vbuf[sb, qi * blk:(qi + 2) * blk, :]
            s = lax.dot_general(q, kk, (((1,), (1,)), ((), ())), preferred_element_type=F32)
            s = jnp.where(band_first if qi == 0 else band, s, NEG)
            m = jnp.max(s, axis=1, keepdims=True)
            p = jnp.exp(s - m)
            l = jnp.sum(p, axis=1, keepdims=True)
            o = jnp.dot(p.astype(BF16), vv, preferred_element_type=F32) / l
            o_ref[sb, qi * blk:(qi + 1) * blk, :] = o.astype(o_ref.dtype)
            lse = m + jnp.log(l)
            lse_ref[sb, qi:qi + 1, :] = jnp.sum(
                jnp.where(eye, jnp.broadcast_to(lse, (blk, blk)), 0.0), axis=0, keepdims=True)


def _band_attention(q, k, v, rows_per_step=2048):
    NS, L, Dh = q.shape
    blk = ATT_BLOCK
    LB = min(L, rows_per_step)
    SB = max(1, rows_per_step // LB)
    assert L % LB == 0 and NS % SB == 0 and LB % blk == 0
    nlb = LB // blk
    cur = pl.BlockSpec((SB, LB, Dh), lambda s, n: (s, n, 0))
    prev = pl.BlockSpec((SB, blk, Dh), lambda s, n: (s, jnp.maximum(n * nlb - 1, 0), 0))
    return pl.pallas_call(
        _attn_body,
        grid=(NS // SB, L // LB),
        in_specs=[cur, cur, prev, cur, prev],
        out_specs=[cur, pl.BlockSpec((SB, nlb, blk), lambda s, n: (s, n, 0))],
        out_shape=[jax.ShapeDtypeStruct((NS, L, Dh), BF16),
                   jax.ShapeDtypeStruct((NS, L // blk, blk), F32)],
        scratch_shapes=[pltpu.VMEM((SB, LB + blk, Dh), BF16), pltpu.VMEM((SB, LB + blk, Dh), BF16)],
        compiler_params=_cparams("parallel", "arbitrary"),
        name="band_attention",
    )(q, k, k, v, v)


def _combine_body(*refs, dils):
    G = len(dils)
    o_refs, l_refs = refs[:G], refs[G:2 * G]
    out_ref, acc_ref, wcol_ref = refs[2 * G:]
    S = out_ref.shape[0]
    blk = LANES
    eye = (lax.broadcasted_iota(jnp.int32, (blk, blk), 0) == lax.broadcasted_iota(jnp.int32, (blk, blk), 1))

    lses = [l[...] for l in l_refs]
    mx = functools.reduce(jnp.maximum, lses)
    es = [jnp.exp(l - mx) for l in lses]
    tot = functools.reduce(lambda a, b: a + b, es)
    for g in range(G):
        mix = es[g] / tot
        for tb in range(S // blk):
            seg = jnp.broadcast_to(mix[:, tb * blk:(tb + 1) * blk], (blk, blk))
            col = jnp.sum(jnp.where(eye, seg, 0.0), axis=1, keepdims=True)
            wcol_ref[g * S + tb * blk:g * S + (tb + 1) * blk, :] = jnp.broadcast_to(col, (blk, blk))
    for g, d in enumerate(dils):
        n = S // d
        for r in range(d):
            rows = pl.ds(r, n, stride=d) if d > 1 else pl.ds(0, n)
            wrows = pl.ds(g * S + r, n, stride=d) if d > 1 else pl.ds(g * S, n)
            contrib = o_refs[g][r].astype(F32) * wcol_ref[wrows, :]
            if g == 0:
                acc_ref[rows, :] = contrib
            else:
                acc_ref[rows, :] += contrib
    out_ref[...] = acc_ref[...].astype(out_ref.dtype)


def _combine(os_, lses, B, S, n_heads):
    dils = tuple(d for _, d in ATT_GROUPS)
    in_specs = [pl.BlockSpec((None, None, d, S // d, HEAD_DIM), lambda b, h: (b, h, 0, 0, 0)) for d in dils]
    in_specs += [pl.BlockSpec((None, None, 1, S), lambda b, h: (b, h, 0, 0))] * len(dils)
    return pl.pallas_call(
        functools.partial(_combine_body, dils=dils),
        grid=(B, n_heads),
        in_specs=in_specs,
        out_specs=pl.BlockSpec((None, S, HEAD_DIM), lambda b, h: (b, 0, h)),
        out_shape=jax.ShapeDtypeStruct((B, S, n_heads * HEAD_DIM), BF16),
        scratch_shapes=[pltpu.VMEM((S, HEAD_DIM), F32), pltpu.VMEM((len(dils) * S, LANES), F32)],
        compiler_params=_cparams("parallel", "parallel"),
        name="group_combine",
    )(*os_, *lses)


def kernel(x, mlstm_w_in, mlstm_b_gate, mlstm_norm_g, mlstm_w_out, att_w_q, att_w_kv_shared, att_w_o,
           router_w, router_b, expert_w_gate, expert_w_up, expert_w_down, ln_g, ln_b):
    B, S, D = x.shape
    T = B * S
    depth = ln_g.shape[0]
    n_a = mlstm_w_in.shape[0]
    alpha = (2.0 * depth) ** 0.25

    Hm = mlstm_b_gate.shape[-1] // 2
    dv = mlstm_w_out.shape[1] // Hm
    dk = (mlstm_w_in.shape[-1] - 2 * Hm - 2 * Hm * dv) // (2 * Hm)
    n_main = 2 * Hm * dk + 2 * Hm * dv

    AH = att_w_o.shape[1] // HEAD_DIM
    G = len(ATT_GROUPS)
    AW = G * AH * HEAD_DIM
    E = router_w.shape[1]
    assert E == N_EXPERT_GROUPS * EXPERTS_PER_GROUP

    perm = (jnp.arange(E) % N_EXPERT_GROUPS) * EXPERTS_PER_GROUP + jnp.arange(E) // N_EXPERT_GROUPS
    router = (router_w.T[perm].astype(F32), router_b[perm].reshape(E, 1).astype(F32))

    h = x.reshape(T, D)
    h_b = h.astype(BF16)
    kv = None
    for layer in range(depth):
        if layer < n_a:
            w_in_t = jnp.swapaxes(mlstm_w_in, 1, 2)
            proj = _mm(h_b, w_in_t, layer, 0, n_main, BF16, w_transposed=True)
            gates_t = _mlstm_gates(h_b, w_in_t[layer, n_main:], mlstm_b_gate[layer])
            hm = _mlstm(proj.reshape(B, S, n_main), gates_t, mlstm_norm_g[layer], B, S, Hm, dk, dv)
            mix = _mm(hm.reshape(T, Hm * dv), mlstm_w_out, layer, 0, D, BF16)
        else:
            bl = layer - n_a
            k_tabs = _rope_tables(S, 1.0)
            q_tabs = _rope_tables(S, HEAD_DIM ** -0.5)
            w_kv = att_w_kv_shared.reshape(1, D, 2 * AW)
            os_, lses = [], []
            for g, (window, d) in enumerate(ATT_GROUPS):
                assert window // d == ATT_BLOCK and (S // d) % ATT_BLOCK == 0
                if kv is None or len(kv) <= g:
                    kv = (kv or []) + [(
                        _proj_heads(h_b, w_kv, 0, g * AH * HEAD_DIM, AH, d, B, S, k_tabs, True),
                        _proj_heads(h_b, w_kv, 0, AW + g * AH * HEAD_DIM, AH, d, B, S, k_tabs, False))]
                kg, vg = kv[g]
                qg = _proj_heads(h_b, att_w_q, bl, g * AH * HEAD_DIM, AH, d, B, S, q_tabs, True)
                L = S // d
                ns = B * AH * d
                o, lse = _band_attention(qg.reshape(ns, L, HEAD_DIM), kg.reshape(ns, L, HEAD_DIM),
                                         vg.reshape(ns, L, HEAD_DIM))
                os_.append(o.reshape(B, AH, d, L, HEAD_DIM))
                lses.append(lse.reshape(B, AH, d, L).transpose(0, 1, 3, 2).reshape(B, AH, 1, S))
            ob = _combine(os_, lses, B, S, AH)
            mix = _mm(ob.reshape(T, AH * HEAD_DIM), att_w_o, bl, 0, D, BF16)

        tm_ln = min(256, T)
        h1_slab, eidx, ew = _ln(T, D, [(h, False, 0), (mix, False, 0)], ln_g[layer, 0], ln_b[layer, 0],
                                alpha, ("slab",), router=router, tm=tm_ln)
        y2 = _moe(h1_slab, eidx, expert_w_gate, expert_w_up, expert_w_down, layer, T, D)
        last = layer == depth - 1
        outs = _ln(T, D, [(h1_slab, True, 0), (y2, True, 0), (y2, True, T // tm_ln)],
                   ln_g[layer, 1], ln_b[layer, 1], alpha, ("f32",) if last else ("f32", "bf16"),
                   add_w=ew, tm=tm_ln)
        h = outs[0]
        h_b = None if last else outs[1]
    return h.reshape(B, S, D)
```

```python
import functools
import math

import jax
import jax.numpy as jnp
from jax import lax
from jax.experimental import pallas as pl
from jax.experimental.pallas import tpu as pltpu

F32 = jnp.float32
BF16 = jnp.bfloat16

HEAD_DIM = 128
ATT_GROUPS = ((128, 1), (512, 4), (2048, 16))
ATT_BLOCK = 128
ROPE_DIM = HEAD_DIM // 4
ROPE_THETA = 500000.0
N_EXPERT_GROUPS = 8
EXPERTS_PER_GROUP = 4
NORM_EPS = 1e-5
MLSTM_CHUNK = 256
LANES = 128
SUBLANES = 8
VMEM_LIMIT = 56 * 1024 * 1024
NEG = -1e30


def _cparams(*sem):
    return pltpu.CompilerParams(dimension_semantics=sem, vmem_limit_bytes=VMEM_LIMIT)


def _mm_body(x_ref, w_ref, o_ref, *, w_transposed):
    contract = (((1,), (1 if w_transposed else 0,)), ((), ()))
    o_ref[...] = lax.dot_general(x_ref[...], w_ref[...].astype(BF16), contract,
                                 preferred_element_type=F32).astype(o_ref.dtype)


def _mm(x, w, layer, col0, ncols, out_dtype, w_transposed=False, tm=1024, tn=512):
    M, K = x.shape
    tm = min(tm, M)
    assert M % tm == 0 and ncols % tn == 0 and col0 % tn == 0
    c0 = col0 // tn
    if w_transposed:
        w_spec = pl.BlockSpec((None, tn, K), lambda i, j: (layer, c0 + j, 0))
    else:
        w_spec = pl.BlockSpec((None, K, tn), lambda i, j: (layer, 0, c0 + j))
    return pl.pallas_call(
        functools.partial(_mm_body, w_transposed=w_transposed),
        grid=(M // tm, ncols // tn),
        in_specs=[pl.BlockSpec((tm, K), lambda i, j: (i, 0)), w_spec],
        out_specs=pl.BlockSpec((tm, tn), lambda i, j: (i, j)),
        out_shape=jax.ShapeDtypeStruct((M, ncols), out_dtype),
        compiler_params=_cparams("parallel", "parallel"),
        name="dense_proj",
    )(x, w)


def _gates_body(wt_ref, x_ref, b_ref, o_ref, *, n_heads):
    g = lax.dot_general(wt_ref[...].astype(BF16), x_ref[...], (((1,), (1,)), ((), ())),
                        preferred_element_type=F32)
    g = g + b_ref[...]
    log_sig = jnp.minimum(g, 0.0) - jnp.log1p(jnp.exp(-jnp.abs(g)))
    row = lax.broadcasted_iota(jnp.int32, g.shape, 0)
    o_ref[...] = jnp.where(row >= n_heads, log_sig, g)


def _mlstm_gates(x, w_gates_t, b_gate, tm=512):
    T, D = x.shape
    H2 = w_gates_t.shape[0]
    tm = min(tm, T)
    return pl.pallas_call(
        functools.partial(_gates_body, n_heads=H2 // 2),
        grid=(T // tm,),
        in_specs=[pl.BlockSpec((H2, D), lambda i: (0, 0)),
                  pl.BlockSpec((tm, D), lambda i: (i, 0)),
                  pl.BlockSpec((H2, 1), lambda i: (0, 0))],
        out_specs=pl.BlockSpec((H2, tm), lambda i: (0, i)),
        out_shape=jax.ShapeDtypeStruct((H2, T), F32),
        compiler_params=_cparams("parallel"),
        name="mlstm_gates",
    )(w_gates_t, x, b_gate.reshape(H2, 1).astype(F32))


def _mlstm_body(q_ref, k_ref, v_ref, op_ref, li_ref, lf_ref, g_ref, out_ref,
                c_ref, n_ref, m_ref, *, scale):
    L = q_ref.shape[0]

    @pl.when(pl.program_id(2) == 0)
    def _():
        c_ref[...] = jnp.zeros_like(c_ref)
        n_ref[...] = jnp.zeros_like(n_ref)
        m_ref[...] = jnp.zeros_like(m_ref)

    q = q_ref[...]
    k = k_ref[...]
    v = v_ref[...]
    li = li_ref[...]
    lf = lf_ref[...]
    m_prev = m_ref[...]

    row = lax.broadcasted_iota(jnp.int32, (L, L), 0)
    col = lax.broadcasted_iota(jnp.int32, (L, L), 1)
    causal = col <= row
    eye = col == row

    def to_col(r):
        return jnp.sum(jnp.where(eye, jnp.broadcast_to(r, (L, L)), 0.0), axis=1, keepdims=True)

    b_col = jnp.sum(jnp.where(causal, jnp.broadcast_to(lf, (L, L)), 0.0), axis=1, keepdims=True)
    b_row = jnp.sum(jnp.where(eye, jnp.broadcast_to(b_col, (L, L)), 0.0), axis=0, keepdims=True)
    a_row = li - b_row
    a_b = jnp.broadcast_to(a_row, (L, L))
    mloc = jnp.maximum(m_prev, jnp.max(jnp.where(causal, a_b, NEG), axis=1, keepdims=True))
    dmat = jnp.exp(jnp.where(causal, a_b - mloc, NEG))
    inter = jnp.exp(m_prev - mloc)

    s = lax.dot_general(q, k, (((1,), (1,)), ((), ())), preferred_element_type=F32) * scale * dmat
    qc = jnp.dot(q, c_ref[...].astype(BF16), preferred_element_type=F32) * scale
    num = inter * qc + jnp.dot(s.astype(BF16), v, preferred_element_type=F32)
    qn = jnp.sum(q.astype(F32) * n_ref[...], axis=1, keepdims=True) * scale
    den = inter * qn + jnp.sum(s, axis=1, keepdims=True)
    m_t = b_col + mloc
    h = num / jnp.maximum(jnp.abs(den), jnp.exp(-m_t))

    m_last = jnp.maximum(m_prev, jnp.max(a_row, axis=1, keepdims=True))
    b_last = jnp.sum(lf, axis=1, keepdims=True)
    decay = jnp.exp(m_prev - m_last)
    w_col = to_col(jnp.exp(a_row - m_last))
    wk = w_col * k.astype(F32)
    c_ref[...] = decay * c_ref[...] + lax.dot_general(
        wk.astype(BF16), v, (((0,), (0,)), ((), ())), preferred_element_type=F32)
    n_ref[...] = decay * n_ref[...] + jnp.sum(wk, axis=0, keepdims=True)
    m_ref[...] = b_last + m_last

    hn = h * lax.rsqrt(jnp.mean(h * h, axis=1, keepdims=True) + NORM_EPS)
    gate = 1.0 / (1.0 + jnp.exp(-op_ref[...].astype(F32)))
    out_ref[...] = (hn * g_ref[...] * gate).astype(out_ref.dtype)


def _mlstm(proj, gates_t, norm_g, B, S, H, dk, dv):
    L = min(MLSTM_CHUNK, S)
    nc = S // L
    assert S % L == 0 and (2 * H * dk) % dv == 0
    v0 = (2 * H * dk) // dv
    gates3 = gates_t.reshape(2 * H, 1, B * S)
    return pl.pallas_call(
        functools.partial(_mlstm_body, scale=dk ** -0.5),
        grid=(B, H, nc),
        in_specs=[pl.BlockSpec((None, L, dk), lambda b, h, c: (b, c, h)),
                  pl.BlockSpec((None, L, dk), lambda b, h, c: (b, c, H + h)),
                  pl.BlockSpec((None, L, dv), lambda b, h, c: (b, c, v0 + h)),
                  pl.BlockSpec((None, L, dv), lambda b, h, c: (b, c, v0 + H + h)),
                  pl.BlockSpec((None, 1, L), lambda b, h, c: (h, 0, b * nc + c)),
                  pl.BlockSpec((None, 1, L), lambda b, h, c: (H + h, 0, b * nc + c)),
                  pl.BlockSpec((1, dv), lambda b, h, c: (0, h))],
        out_specs=pl.BlockSpec((None, L, dv), lambda b, h, c: (b, c, h)),
        out_shape=jax.ShapeDtypeStruct((B, S, H * dv), BF16),
        scratch_shapes=[pltpu.VMEM((dk, dv), F32), pltpu.VMEM((1, dk), F32), pltpu.VMEM((1, 1), F32)],
        compiler_params=_cparams("parallel", "parallel", "arbitrary"),
        name="mlstm_chunks",
    )(proj, proj, proj, proj, gates3, gates3, norm_g.reshape(1, H * dv).astype(F32))


def _slab_pitch(D):
    return D // LANES + SUBLANES


def _slab_load(ref, n_tok, D):
    pitch = _slab_pitch(D)
    return jnp.concatenate([ref[pl.ds(c, n_tok, stride=pitch), :] for c in range(D // LANES)], axis=1)


def _slab_store(ref, val, pad=True):
    n_tok, D = val.shape
    pitch = _slab_pitch(D)
    nrow = D // LANES
    for c in range(nrow):
        ref[pl.ds(c, n_tok, stride=pitch), :] = val[:, c * LANES:(c + 1) * LANES].astype(ref.dtype)
    if pad:
        for c in range(nrow, pitch):
            ref[pl.ds(c, n_tok, stride=pitch), :] = jnp.zeros((n_tok, LANES), ref.dtype)


def _ln_body(*refs, tm, slabs, weighted, alpha, router, outs):
    n_in = len(slabs)
    pos = n_in
    if weighted:
        aw = refs[pos][...]
        pos += 1
    g_ref, b_ref = refs[pos], refs[pos + 1]
    pos += 2
    D = g_ref.shape[1]
    vals = [_slab_load(r, tm, D) if sl else r[...].astype(F32) for r, sl in zip(refs[:n_in], slabs)]
    if router:
        wr_ref, br_ref = refs[pos], refs[pos + 1]
        pos += 2

    z = alpha * vals[0]
    for k, a in enumerate(vals[1:]):
        z = z + (a * aw[:, k:k + 1] if weighted else a)
    mu = jnp.mean(z, axis=1, keepdims=True)
    zc = z - mu
    var = jnp.mean(zc * zc, axis=1, keepdims=True)
    y = zc * lax.rsqrt(var + NORM_EPS) * g_ref[...] + b_ref[...]
    for kind in outs:
        if kind == "slab":
            _slab_store(refs[pos], y)
        else:
            refs[pos][...] = y.astype(refs[pos].dtype)
        pos += 1

    if router:
        ei_ref, ew_ref = refs[pos], refs[pos + 1]
        ng, epg = N_EXPERT_GROUPS, EXPERTS_PER_GROUP
        lt = lax.dot_general(wr_ref[...], y, (((1,), (1,)), ((), ())),
                             precision=lax.Precision.HIGHEST, preferred_element_type=F32) + br_ref[...]
        ex = jnp.exp(lt - jnp.max(lt, axis=0, keepdims=True))
        p = ex / jnp.sum(ex, axis=0, keepdims=True)
        p0, p1, p2, p3 = [p[j * ng:(j + 1) * ng] for j in range(epg)]
        a, b = jnp.maximum(p0, p1), jnp.minimum(p0, p1)
        c, d = jnp.maximum(p2, p3), jnp.minimum(p2, p3)
        score = jnp.maximum(a, c) + jnp.maximum(jnp.minimum(a, c), jnp.maximum(b, d))
        gi = lax.broadcasted_iota(jnp.int32, score.shape, 0)
        best = jnp.min(jnp.where(score == jnp.max(score, axis=0, keepdims=True), gi, ng),
                       axis=0, keepdims=True)
        sel = gi == best
        v = [jnp.sum(jnp.where(sel, pj, 0.0), axis=0, keepdims=True) for pj in (p0, p1, p2, p3)]

        def first_max(vals):
            top = jnp.maximum(jnp.maximum(vals[0], vals[1]), jnp.maximum(vals[2], vals[3]))
            idx = jnp.where(vals[0] == top, 0, jnp.where(vals[1] == top, 1, jnp.where(vals[2] == top, 2, 3)))
            return top, idx

        v1, j1 = first_max(v)
        v2, j2 = first_max([jnp.where(j1 == j, -1.0, v[j]) for j in range(epg)])
        tot = v1 + v2
        ei_ref[...] = jnp.concatenate([best * epg + j1, best * epg + j2], axis=0).astype(jnp.int32)
        eye = (lax.broadcasted_iota(jnp.int32, (tm, tm), 0) == lax.broadcasted_iota(jnp.int32, (tm, tm), 1))
        cols = [jnp.sum(jnp.where(eye, jnp.broadcast_to(w, (tm, tm)), 0.0), axis=1, keepdims=True)
                for w in (v1 / tot, v2 / tot)]
        lane = lax.broadcasted_iota(jnp.int32, (tm, LANES), 1)
        ew_ref[...] = jnp.where(lane == 0, cols[0], jnp.where(lane == 1, cols[1], 0.0))


def _ln(T, D, inputs, g, b, alpha, outs, add_w=None, router=None, tm=256):
    tm = min(tm, T)
    pitch = _slab_pitch(D)

    def spec(is_slab, off):
        if is_slab:
            return pl.BlockSpec((tm * pitch, LANES), lambda i: (off + i, 0))
        return pl.BlockSpec((tm, D), lambda i: (off + i, 0))

    in_specs = [spec(sl, off) for _, sl, off in inputs]
    args = [a for a, _, _ in inputs]
    if add_w is not None:
        in_specs.append(pl.BlockSpec((tm, LANES), lambda i: (i, 0)))
        args.append(add_w)
    in_specs += [pl.BlockSpec((1, D), lambda i: (0, 0))] * 2
    args += [g.reshape(1, D).astype(F32), b.reshape(1, D).astype(F32)]
    out_shape, out_specs = [], []
    for kind in outs:
        if kind == "slab":
            out_shape.append(jax.ShapeDtypeStruct((T * pitch, LANES), F32))
        else:
            out_shape.append(jax.ShapeDtypeStruct((T, D), F32 if kind == "f32" else BF16))
        out_specs.append(spec(kind == "slab", 0))
    if router is not None:
        wr_t, br = router
        E = wr_t.shape[0]
        in_specs += [pl.BlockSpec((E, D), lambda i: (0, 0)), pl.BlockSpec((E, 1), lambda i: (0, 0))]
        args += [wr_t, br]
        out_shape += [jax.ShapeDtypeStruct((2, T), jnp.int32), jax.ShapeDtypeStruct((T, LANES), F32)]
        out_specs += [pl.BlockSpec((2, tm), lambda i: (0, i)), pl.BlockSpec((tm, LANES), lambda i: (i, 0))]
    return pl.pallas_call(
        functools.partial(_ln_body, tm=tm, slabs=tuple(sl for _, sl, _ in inputs),
                          weighted=add_w is not None, alpha=alpha,
                          router=router is not None, outs=tuple(outs)),
        grid=(T // tm,),
        in_specs=in_specs, out_specs=out_specs, out_shape=out_shape,
        compiler_params=_cparams("parallel"),
        name="deepnorm_ln_router" if router is not None else "deepnorm_ln",
    )(*args)


FFN_SUB = 256
FFN_VISIT_SUBS = 3


def _ffn_body(ve_ref, vrow_ref, vcnt_ref, nv_ref, ord_ref, tok_ref, h_hbm, hflat_hbm, wg_ref, wu_ref, wd_ref,
              y_hbm, gin_ref, gout_ref, xb_ref, acc_ref, pend_ref, sem_in, sem_out, *, D):
    v, j = pl.program_id(0), pl.program_id(1)
    n_fc = pl.num_programs(1)
    last_j = n_fc - 1
    pitch = _slab_pitch(D)
    n_data = D // LANES
    sub = FFN_SUB
    used = v < nv_ref[0]
    cnt, row0 = vcnt_ref[v], vrow_ref[v]

    def slab_row(ref, r):
        return ref.at[pl.ds(pl.multiple_of(r * pitch, SUBLANES), n_data)]

    def wait_rows(n, row_copy, rows_per_copy, vmem_ref, sem):
        @pl.when(n == sub)
        def _():
            whole = pl.ds(0, sub * rows_per_copy)
            pltpu.make_async_copy(hflat_hbm.at[whole], vmem_ref.at[whole], sem).wait()

        @pl.when(n < sub)
        def _():
            def body(r, carry):
                row_copy(r).wait()
                return carry
            lax.fori_loop(0, n, body, 0)

    def start_fetch(first_row, n):
        def body(r, carry):
            src = h_hbm.at[tok_ref[first_row + r], pl.ds(0, n_data)]
            pltpu.make_async_copy(src, slab_row(gin_ref, r), sem_in).start(priority=1)
            return carry
        lax.fori_loop(0, n, body, 0)

    def wait_fetch(n):
        wait_rows(n, lambda r: pltpu.make_async_copy(h_hbm.at[0, pl.ds(0, n_data)], slab_row(gin_ref, r), sem_in),
                  n_data, gin_ref, sem_in)

    def out_row(r):
        return gout_ref.at[pl.ds(pl.multiple_of(r * pitch, SUBLANES), pitch)]

    def start_send(first_row, n):
        def body(r, carry):
            pltpu.make_async_copy(out_row(r), y_hbm.at[ord_ref[first_row + r]], sem_out).start(priority=1)
            return carry
        lax.fori_loop(0, n, body, 0)

    def wait_sends():
        wait_rows(pend_ref[0], lambda r: pltpu.make_async_copy(out_row(r), y_hbm.at[0], sem_out),
                  pitch, gout_ref, sem_out)
        pend_ref[0] = 0

    @pl.when((v == 0) & (j == 0))
    def _():
        gin_ref[...] = jnp.zeros_like(gin_ref)
        gout_ref[...] = jnp.zeros_like(gout_ref)
        xb_ref[...] = jnp.zeros_like(xb_ref)
        pend_ref[0] = 0
        start_fetch(row0, jnp.minimum(cnt, sub))

    for s in range(FFN_VISIT_SUBS):
        n_rows = jnp.minimum(cnt - s * sub, sub)
        active = used & (s * sub < cnt)
        rows = slice(s * sub, (s + 1) * sub)

        @pl.when(active & (j == 0))
        def _():
            wait_fetch(n_rows)
            xb_ref[rows, :] = _slab_load(gin_ref, sub, D).astype(BF16)
            if s + 1 < FFN_VISIT_SUBS:
                @pl.when((s + 1) * sub < cnt)
                def _():
                    start_fetch(row0 + (s + 1) * sub, jnp.minimum(cnt - (s + 1) * sub, sub))

        @pl.when(active)
        def _():
            x = xb_ref[rows, :]
            g = jnp.dot(x, wg_ref[...].astype(BF16), preferred_element_type=F32)
            u = jnp.dot(x, wu_ref[...].astype(BF16), preferred_element_type=F32)
            hmid = g * (1.0 / (1.0 + jnp.exp(-g))) * u
            part = jnp.dot(hmid.astype(BF16), wd_ref[...].astype(BF16), preferred_element_type=F32)

            @pl.when(j == 0)
            def _():
                acc_ref[rows, :] = part

            @pl.when(j > 0)
            def _():
                acc_ref[rows, :] += part

        @pl.when(active & (j == last_j))
        def _():
            wait_sends()
            _slab_store(gout_ref, acc_ref[rows, :], pad=False)
            start_send(row0 + s * sub, n_rows)
            pend_ref[0] = n_rows

    @pl.when((j == jnp.minimum(1, last_j)) & (v + 1 < nv_ref[0]))
    def _():
        start_fetch(vrow_ref[v + 1], jnp.minimum(vcnt_ref[v + 1], sub))

    @pl.when((v == pl.num_programs(0) - 1) & (j == last_j))
    def _():
        wait_sends()


def _moe(h_slab, eidx, w_gate, w_up, w_down, layer, T, D):
    E, F = w_gate.shape[1], w_gate.shape[-1]
    A = 2 * T
    pitch = _slab_pitch(D)
    fc = 256 if F % 256 == 0 else F
    visit_rows = FFN_VISIT_SUBS * FFN_SUB
    n_visits = E + A // visit_rows

    e_flat = eidx.reshape(A)
    order = jnp.argsort(e_flat, stable=True).astype(jnp.int32)
    experts = jnp.arange(E, dtype=jnp.int32)
    counts = jnp.sum((e_flat[:, None] == experts[None, :]).astype(jnp.int32), axis=0)
    start = jnp.cumsum(counts) - counts
    visits_e = (counts + visit_rows - 1) // visit_rows
    visit_end = jnp.cumsum(visits_e)
    n_used = visit_end[-1:].astype(jnp.int32)
    vid = jnp.minimum(jnp.arange(n_visits, dtype=jnp.int32), n_used[0] - 1)
    visit_expert = jnp.sum((vid[:, None] >= visit_end[None, :]).astype(jnp.int32), axis=1).astype(jnp.int32)
    k_in_expert = vid - (visit_end - visits_e)[visit_expert]
    visit_row0 = (start[visit_expert] + k_in_expert * visit_rows).astype(jnp.int32)
    visit_cnt = jnp.minimum(counts[visit_expert] - k_in_expert * visit_rows, visit_rows).astype(jnp.int32)

    n_fc = F // fc

    def chunk(v, j, nv):
        return jnp.where(v < nv[0], j, n_fc - 1)

    def w_map(v, j, ve, vrow, vcnt, nv, ordr, tok):
        return (layer, ve[v], 0, chunk(v, j, nv))

    out = pl.pallas_call(
        functools.partial(_ffn_body, D=D),
        grid_spec=pltpu.PrefetchScalarGridSpec(
            num_scalar_prefetch=6, grid=(n_visits, n_fc),
            in_specs=[pl.BlockSpec(memory_space=pl.ANY),
                      pl.BlockSpec(memory_space=pl.ANY),
                      pl.BlockSpec((None, None, D, fc), w_map),
                      pl.BlockSpec((None, None, D, fc), w_map),
                      pl.BlockSpec((None, None, fc, D),
                                   lambda v, j, ve, vrow, vcnt, nv, ordr, tok: (layer, ve[v], chunk(v, j, nv), 0))],
            out_specs=pl.BlockSpec(memory_space=pl.ANY),
            scratch_shapes=[pltpu.VMEM((FFN_SUB * pitch, LANES), F32), pltpu.VMEM((FFN_SUB * pitch, LANES), F32),
                            pltpu.VMEM((visit_rows, D), BF16), pltpu.VMEM((visit_rows, D), F32),
                            pltpu.SMEM((1,), jnp.int32), pltpu.SemaphoreType.DMA(()),
                            pltpu.SemaphoreType.DMA(())]),
        out_shape=jax.ShapeDtypeStruct((A, pitch, LANES), F32),
        compiler_params=_cparams("arbitrary", "arbitrary"),
        name="expert_ffn",
    )(visit_expert, visit_row0, visit_cnt, n_used, order, jnp.where(order >= T, order - T, order),
      h_slab.reshape(T, pitch, LANES), h_slab, w_gate, w_up, w_down)
    return out.reshape(A * pitch, LANES)


def _proj_heads_body(x_ref, w_ref, ca_ref, sb_ref, sc_ref, o_ref, acc_ref, *, d, hb, rope):
    acc = jnp.dot(x_ref[...], w_ref[...].astype(BF16), preferred_element_type=F32)
    tm = x_ref.shape[0]
    n = tm // d
    half = ROPE_DIM // 2
    if d > 1:
        for hh in range(hb):
            acc_ref[hh * tm:(hh + 1) * tm, :] = acc[:, hh * HEAD_DIM:(hh + 1) * HEAD_DIM]
    for r in range(d):
        if rope:
            rows = pl.ds(r, n, stride=d) if d > 1 else pl.ds(0, n)
            ca, sb, sc = ca_ref[rows, :], sb_ref[rows, :], sc_ref[rows, :]
        for hh in range(hb):
            if d > 1:
                val = acc_ref[pl.ds(hh * tm + r, n, stride=d), :]
            else:
                val = acc[:, hh * HEAD_DIM:(hh + 1) * HEAD_DIM]
            if rope:
                val = (val * ca + pltpu.roll(val, half, 1) * sb
                       + pltpu.roll(val, HEAD_DIM - half, 1) * sc)
            o_ref[hh, r] = val.astype(o_ref.dtype)


def _proj_heads(x, w, layer, col0, n_heads, d, B, S, tables, rope, tm=1024, hb=4):
    M, K = x.shape
    tm = min(tm, S)
    tn = hb * HEAD_DIM
    assert S % tm == 0 and n_heads % hb == 0 and col0 % tn == 0 and (tm // d) % 16 == 0
    mt = S // tm
    c0 = col0 // tn
    tab_spec = pl.BlockSpec((tm, HEAD_DIM), lambda i, j: (i % mt, 0))
    return pl.pallas_call(
        functools.partial(_proj_heads_body, d=d, hb=hb, rope=rope),
        grid=(M // tm, n_heads // hb),
        in_specs=[pl.BlockSpec((tm, K), lambda i, j: (i, 0)),
                  pl.BlockSpec((None, K, tn), lambda i, j: (layer, 0, c0 + j)),
                  tab_spec, tab_spec, tab_spec],
        out_specs=pl.BlockSpec((None, hb, d, tm // d, HEAD_DIM), lambda i, j: (i // mt, j, 0, i % mt, 0)),
        out_shape=jax.ShapeDtypeStruct((B, n_heads, d, S // d, HEAD_DIM), BF16),
        scratch_shapes=[pltpu.VMEM((hb * tm, HEAD_DIM), F32)],
        compiler_params=_cparams("parallel", "parallel"),
        name="attn_proj_d%d" % d,
    )(x, w, *tables)


def _rope_tables(S, scale):
    half = ROPE_DIM // 2
    inv_freq = ROPE_THETA ** (-jnp.arange(half, dtype=F32) / half)
    ang = jnp.arange(S).astype(F32)[:, None] * inv_freq[None, :]
    cos, sin = jnp.cos(ang), jnp.sin(ang)
    zeros = jnp.zeros((S, HEAD_DIM - ROPE_DIM), F32)
    ca = jnp.concatenate([cos, cos, jnp.ones_like(zeros)], axis=1)
    sb = jnp.concatenate([jnp.zeros_like(sin), sin, zeros], axis=1)
    sc = jnp.concatenate([-sin, jnp.zeros_like(sin), zeros], axis=1)
    return ca * scale, sb * scale, sc * scale


def _attn_body(q_ref, kc_ref, kp_ref, vc_ref, vp_ref, o_ref, lse_ref, kbuf, vbuf):
    SB, LB, _ = q_ref.shape
    blk = ATT_BLOCK
    n = pl.program_id(1)
    kbuf[:, :blk] = kp_ref[...]
    kbuf[:, blk:] = kc_ref[...]
    vbuf[:, :blk] = vp_ref[...]
    vbuf[:, blk:] = vc_ref[...]

    row = lax.broadcasted_iota(jnp.int32, (blk, 2 * blk), 0)
    col = lax.broadcasted_iota(jnp.int32, (blk, 2 * blk), 1)
    band = (col >= row) & (col <= row + blk)
    band_first = band & ((col >= blk) | (n > 0))
    eye = (lax.broadcasted_iota(jnp.int32, (blk, blk), 0) == lax.broadcasted_iota(jnp.int32, (blk, blk), 1))

    for sb in range(SB):
        for qi in range(LB // blk):
            q = q_ref[sb, qi * blk:(qi + 1) * blk, :]
            kk = kbuf[sb, qi * blk:(qi + 2) * blk, :]
            vv = vbuf[sb, qi * blk:(qi + 2) * blk, :]
            s = lax.dot_general(q, kk, (((1,), (1,)), ((), ())), preferred_element_type=F32)
            s = jnp.where(band_first if qi == 0 else band, s, NEG)
            m = jnp.max(s, axis=1, keepdims=True)
            p = jnp.exp(s - m)
            l = jnp.sum(p, axis=1, keepdims=True)
            o = jnp.dot(p.astype(BF16), vv, preferred_element_type=F32) / l
            o_ref[sb, qi * blk:(qi + 1) * blk, :] = o.astype(o_ref.dtype)
            lse = m + jnp.log(l)
            lse_ref[sb, qi:qi + 1, :] = jnp.sum(
                jnp.where(eye, jnp.broadcast_to(lse, (blk, blk)), 0.0), axis=0, keepdims=True)


def _band_attention(q, k, v, rows_per_step=2048):
    NS, L, Dh = q.shape
    blk = ATT_BLOCK
    LB = min(L, rows_per_step)
    SB = max(1, rows_per_step // LB)
    assert L % LB == 0 and NS % SB == 0 and LB % blk == 0
    nlb = LB // blk
    cur = pl.BlockSpec((SB, LB, Dh), lambda s, n: (s, n, 0))
    prev = pl.BlockSpec((SB, blk, Dh), lambda s, n: (s, jnp.maximum(n * nlb - 1, 0), 0))
    return pl.pallas_call(
        _attn_body,
        grid=(NS // SB, L // LB),
        in_specs=[cur, cur, prev, cur, prev],
        out_specs=[cur, pl.BlockSpec((SB, nlb, blk), lambda s, n: (s, n, 0))],
        out_shape=[jax.ShapeDtypeStruct((NS, L, Dh), BF16),
                   jax.ShapeDtypeStruct((NS, L // blk, blk), F32)],
        scratch_shapes=[pltpu.VMEM((SB, LB + blk, Dh), BF16), pltpu.VMEM((SB, LB + blk, Dh), BF16)],
        compiler_params=_cparams("parallel", "arbitrary"),
        name="band_attention",
    )(q, k, k, v, v)


def _combine_body(*refs, dils):
    G = len(dils)
    o_refs, l_refs = refs[:G], refs[G:2 * G]
    out_ref, acc_ref, wcol_ref = refs[2 * G:]
    S = out_ref.shape[0]
    blk = LANES
    eye = (lax.broadcasted_iota(jnp.int32, (blk, blk), 0) == lax.broadcasted_iota(jnp.int32, (blk, blk), 1))

    lses = [l[...] for l in l_refs]
    mx = functools.reduce(jnp.maximum, lses)
    es = [jnp.exp(l - mx) for l in lses]
    tot = functools.reduce(lambda a, b: a + b, es)
    for g in range(G):
        mix = es[g] / tot
        for tb in range(S // blk):
            seg = jnp.broadcast_to(mix[:, tb * blk:(tb + 1) * blk], (blk, blk))
            col = jnp.sum(jnp.where(eye, seg, 0.0), axis=1, keepdims=True)
            wcol_ref[g * S + tb * blk:g * S + (tb + 1) * blk, :] = jnp.broadcast_to(col, (blk, blk))
    for g, d in enumerate(dils):
        n = S // d
        for r in range(d):
            rows = pl.ds(r, n, stride=d) if d > 1 else pl.ds(0, n)
            wrows = pl.ds(g * S + r, n, stride=d) if d > 1 else pl.ds(g * S, n)
            contrib = o_refs[g][r].astype(F32) * wcol_ref[wrows, :]
            if g == 0:
                acc_ref[rows, :] = contrib
            else:
                acc_ref[rows, :] += contrib
    out_ref[...] = acc_ref[...].astype(out_ref.dtype)


def _combine(os_, lses, B, S, n_heads):
    dils = tuple(d for _, d in ATT_GROUPS)
    in_specs = [pl.BlockSpec((None, None, d, S // d, HEAD_DIM), lambda b, h: (b, h, 0, 0, 0)) for d in dils]
    in_specs += [pl.BlockSpec((None, None, 1, S), lambda b, h: (b, h, 0, 0))] * len(dils)
    return pl.pallas_call(
        functools.partial(_combine_body, dils=dils),
        grid=(B, n_heads),
        in_specs=in_specs,
        out_specs=pl.BlockSpec((None, S, HEAD_DIM), lambda b, h: (b, 0, h)),
        out_shape=jax.ShapeDtypeStruct((B, S, n_heads * HEAD_DIM), BF16),
        scratch_shapes=[pltpu.VMEM((S, HEAD_DIM), F32), pltpu.VMEM((len(dils) * S, LANES), F32)],
        compiler_params=_cparams("parallel", "parallel"),
        name="group_combine",
    )(*os_, *lses)


def kernel(x, mlstm_w_in, mlstm_b_gate, mlstm_norm_g, mlstm_w_out, att_w_q, att_w_kv_shared, att_w_o,
           router_w, router_b, expert_w_gate, expert_w_up, expert_w_down, ln_g, ln_b):
    B, S, D = x.shape
    T = B * S
    depth = ln_g.shape[0]
    n_a = mlstm_w_in.shape[0]
    alpha = (2.0 * depth) ** 0.25

    Hm = mlstm_b_gate.shape[-1] // 2
    dv = mlstm_w_out.shape[1] // Hm
    dk = (mlstm_w_in.shape[-1] - 2 * Hm - 2 * Hm * dv) // (2 * Hm)
    n_main = 2 * Hm * dk + 2 * Hm * dv

    AH = att_w_o.shape[1] // HEAD_DIM
    G = len(ATT_GROUPS)
    AW = G * AH * HEAD_DIM
    E = router_w.shape[1]
    assert E == N_EXPERT_GROUPS * EXPERTS_PER_GROUP

    perm = (jnp.arange(E) % N_EXPERT_GROUPS) * EXPERTS_PER_GROUP + jnp.arange(E) // N_EXPERT_GROUPS
    router = (router_w.T[perm].astype(F32), router_b[perm].reshape(E, 1).astype(F32))

    h = x.reshape(T, D)
    h_b = h.astype(BF16)
    kv = None
    for layer in range(depth):
        if layer < n_a:
            w_in_t = jnp.swapaxes(mlstm_w_in, 1, 2)
            proj = _mm(h_b, w_in_t, layer, 0, n_main, BF16, w_transposed=True)
            gates_t = _mlstm_gates(h_b, w_in_t[layer, n_main:], mlstm_b_gate[layer])
            hm = _mlstm(proj.reshape(B, S, n_main), gates_t, mlstm_norm_g[layer], B, S, Hm, dk, dv)
            mix = _mm(hm.reshape(T, Hm * dv), mlstm_w_out, layer, 0, D, BF16)
        else:
            bl = layer - n_a
            k_tabs = _rope_tables(S, 1.0)
            q_tabs = _rope_tables(S, HEAD_DIM ** -0.5)
            w_kv = att_w_kv_shared.reshape(1, D, 2 * AW)
            os_, lses = [], []
            for g, (window, d) in enumerate(ATT_GROUPS):
                assert window // d == ATT_BLOCK and (S // d) % ATT_BLOCK == 0
                if kv is None or len(kv) <= g:
                    kv = (kv or []) + [(
                        _proj_heads(h_b, w_kv, 0, g * AH * HEAD_DIM, AH, d, B, S, k_tabs, True),
                        _proj_heads(h_b, w_kv, 0, AW + g * AH * HEAD_DIM, AH, d, B, S, k_tabs, False))]
                kg, vg = kv[g]
                qg = _proj_heads(h_b, att_w_q, bl, g * AH * HEAD_DIM, AH, d, B, S, q_tabs, True)
                L = S // d
                ns = B * AH * d
                o, lse = _band_attention(qg.reshape(ns, L, HEAD_DIM), kg.reshape(ns, L, HEAD_DIM),
                                         vg.reshape(ns, L, HEAD_DIM))
                os_.append(o.reshape(B, AH, d, L, HEAD_DIM))
                lses.append(lse.reshape(B, AH, d, L).transpose(0, 1, 3, 2).reshape(B, AH, 1, S))
            ob = _combine(os_, lses, B, S, AH)
            mix = _mm(ob.reshape(T, AH * HEAD_DIM), att_w_o, bl, 0, D, BF16)

        tm_ln = min(256, T)
        h1_slab, eidx, ew = _ln(T, D, [(h, False, 0), (mix, False, 0)], ln_g[layer, 0], ln_b[layer, 0],
                                alpha, ("slab",), router=router, tm=tm_ln)
        y2 = _moe(h1_slab, eidx, expert_w_gate, expert_w_up, expert_w_down, layer, T, D)
        last = layer == depth - 1
        outs = _ln(T, D, [(h1_slab, True, 0), (y2, True, 0), (y2, True, T // tm_ln)],
                   ln_g[layer, 1], ln_b[layer, 1], alpha, ("f32",) if last else ("f32", "bf16"),
                   add_w=ew, tm=tm_ln)
        h = outs[0]
        h_b = None if last else outs[1]
    return h.reshape(B, S, D)
```

```python
import functools
import math

import jax
import jax.numpy as jnp
from jax import lax
from jax.experimental import pallas as pl
from jax.experimental.pallas import tpu as pltpu

F32 = jnp.float32
BF16 = jnp.bfloat16

HEAD_DIM = 128
ATT_GROUPS = ((128, 1), (512, 4), (2048, 16))
ATT_BLOCK = 128
ROPE_DIM = HEAD_DIM // 4
ROPE_THETA = 500000.0
N_EXPERT_GROUPS = 8
EXPERTS_PER_GROUP = 4
NORM_EPS = 1e-5
MLSTM_CHUNK = 256
LANES = 128
SUBLANES = 8
VMEM_LIMIT = 56 * 1024 * 1024
NEG = -1e30


def _cparams(*sem, vmem_limit=VMEM_LIMIT):
    return pltpu.CompilerParams(dimension_semantics=sem, vmem_limit_bytes=vmem_limit)


def _mm_body(x_ref, w_ref, o_ref, *, w_transposed):
    contract = (((1,), (1 if w_transposed else 0,)), ((), ()))
    o_ref[...] = lax.dot_general(x_ref[...], w_ref[...].astype(BF16), contract,
                                 preferred_element_type=F32).astype(o_ref.dtype)


def _mm(x, w, layer, col0, ncols, out_dtype, w_transposed=False, tm=1024, tn=512):
    M, K = x.shape
    tm = min(tm, M)
    assert M % tm == 0 and ncols % tn == 0 and col0 % tn == 0
    c0 = col0 // tn
    if w_transposed:
        w_spec = pl.BlockSpec((None, tn, K), lambda i, j: (layer, c0 + j, 0))
    else:
        w_spec = pl.BlockSpec((None, K, tn), lambda i, j: (layer, 0, c0 + j))
    return pl.pallas_call(
        functools.partial(_mm_body, w_transposed=w_transposed),
        grid=(M // tm, ncols // tn),
        in_specs=[pl.BlockSpec((tm, K), lambda i, j: (i, 0)), w_spec],
        out_specs=pl.BlockSpec((tm, tn), lambda i, j: (i, j)),
        out_shape=jax.ShapeDtypeStruct((M, ncols), out_dtype),
        compiler_params=_cparams("parallel", "parallel"),
        name="dense_proj",
    )(x, w)


def _gates_body(wt_ref, x_ref, b_ref, o_ref, *, n_heads):
    g = lax.dot_general(wt_ref[...].astype(BF16), x_ref[...], (((1,), (1,)), ((), ())),
                        preferred_element_type=F32)
    g = g + b_ref[...]
    log_sig = jnp.minimum(g, 0.0) - jnp.log1p(jnp.exp(-jnp.abs(g)))
    row = lax.broadcasted_iota(jnp.int32, g.shape, 0)
    o_ref[...] = jnp.where(row >= n_heads, log_sig, g)


def _mlstm_gates(x, w_gates_t, b_gate, tm=512):
    T, D = x.shape
    H2 = w_gates_t.shape[0]
    tm = min(tm, T)
    return pl.pallas_call(
        functools.partial(_gates_body, n_heads=H2 // 2),
        grid=(T // tm,),
        in_specs=[pl.BlockSpec((H2, D), lambda i: (0, 0)),
                  pl.BlockSpec((tm, D), lambda i: (i, 0)),
                  pl.BlockSpec((H2, 1), lambda i: (0, 0))],
        out_specs=pl.BlockSpec((H2, tm), lambda i: (0, i)),
        out_shape=jax.ShapeDtypeStruct((H2, T), F32),
        compiler_params=_cparams("parallel"),
        name="mlstm_gates",
    )(w_gates_t, x, b_gate.reshape(H2, 1).astype(F32))


def _mlstm_body(q_ref, k_ref, v_ref, op_ref, li_ref, lf_ref, g_ref, out_ref,
                c_ref, n_ref, m_ref, *, scale):
    L = q_ref.shape[0]

    @pl.when(pl.program_id(2) == 0)
    def _():
        c_ref[...] = jnp.zeros_like(c_ref)
        n_ref[...] = jnp.zeros_like(n_ref)
        m_ref[...] = jnp.zeros_like(m_ref)

    q = q_ref[...]
    k = k_ref[...]
    v = v_ref[...]
    li = li_ref[...]
    lf = lf_ref[...]
    m_prev = m_ref[...]

    row = lax.broadcasted_iota(jnp.int32, (L, L), 0)
    col = lax.broadcasted_iota(jnp.int32, (L, L), 1)
    causal = col <= row
    eye = col == row

    def to_col(r):
        return jnp.sum(jnp.where(eye, jnp.broadcast_to(r, (L, L)), 0.0), axis=1, keepdims=True)

    b_col = jnp.sum(jnp.where(causal, jnp.broadcast_to(lf, (L, L)), 0.0), axis=1, keepdims=True)
    b_row = jnp.sum(jnp.where(eye, jnp.broadcast_to(b_col, (L, L)), 0.0), axis=0, keepdims=True)
    a_row = li - b_row
    a_b = jnp.broadcast_to(a_row, (L, L))
    mloc = jnp.maximum(m_prev, jnp.max(jnp.where(causal, a_b, NEG), axis=1, keepdims=True))
    dmat = jnp.exp(jnp.where(causal, a_b - mloc, NEG))
    inter = jnp.exp(m_prev - mloc)

    s = lax.dot_general(q, k, (((1,), (1,)), ((), ())), preferred_element_type=F32) * scale * dmat
    qc = jnp.dot(q, c_ref[...].astype(BF16), preferred_element_type=F32) * scale
    num = inter * qc + jnp.dot(s.astype(BF16), v, preferred_element_type=F32)
    qn = jnp.sum(q.astype(F32) * n_ref[...], axis=1, keepdims=True) * scale
    den = inter * qn + jnp.sum(s, axis=1, keepdims=True)
    m_t = b_col + mloc
    h = num / jnp.maximum(jnp.abs(den), jnp.exp(-m_t))

    m_last = jnp.maximum(m_prev, jnp.max(a_row, axis=1, keepdims=True))
    b_last = jnp.sum(lf, axis=1, keepdims=True)
    decay = jnp.exp(m_prev - m_last)
    w_col = to_col(jnp.exp(a_row - m_last))
    wk = w_col * k.astype(F32)
    c_ref[...] = decay * c_ref[...] + lax.dot_general(
        wk.astype(BF16), v, (((0,), (0,)), ((), ())), preferred_element_type=F32)
    n_ref[...] = decay * n_ref[...] + jnp.sum(wk, axis=0, keepdims=True)
    m_ref[...] = b_last + m_last

    hn = h * lax.rsqrt(jnp.mean(h * h, axis=1, keepdims=True) + NORM_EPS)
    gate = 1.0 / (1.0 + jnp.exp(-op_ref[...].astype(F32)))
    out_ref[...] = (hn * g_ref[...] * gate).astype(out_ref.dtype)


def _mlstm(proj, gates_t, norm_g, B, S, H, dk, dv):
    L = min(MLSTM_CHUNK, S)
    nc = S // L
    assert S % L == 0 and (2 * H * dk) % dv == 0
    v0 = (2 * H * dk) // dv
    gates3 = gates_t.reshape(2 * H, 1, B * S)
    return pl.pallas_call(
        functools.partial(_mlstm_body, scale=dk ** -0.5),
        grid=(B, H, nc),
        in_specs=[pl.BlockSpec((None, L, dk), lambda b, h, c: (b, c, h)),
                  pl.BlockSpec((None, L, dk), lambda b, h, c: (b, c, H + h)),
                  pl.BlockSpec((None, L, dv), lambda b, h, c: (b, c, v0 + h)),
                  pl.BlockSpec((None, L, dv), lambda b, h, c: (b, c, v0 + H + h)),
                  pl.BlockSpec((None, 1, L), lambda b, h, c: (h, 0, b * nc + c)),
                  pl.BlockSpec((None, 1, L), lambda b, h, c: (H + h, 0, b * nc + c)),
                  pl.BlockSpec((1, dv), lambda b, h, c: (0, h))],
        out_specs=pl.BlockSpec((None, L, dv), lambda b, h, c: (b, c, h)),
        out_shape=jax.ShapeDtypeStruct((B, S, H * dv), BF16),
        scratch_shapes=[pltpu.VMEM((dk, dv), F32), pltpu.VMEM((1, dk), F32), pltpu.VMEM((1, 1), F32)],
        compiler_params=_cparams("parallel", "parallel", "arbitrary"),
        name="mlstm_chunks",
    )(proj, proj, proj, proj, gates3, gates3, norm_g.reshape(1, H * dv).astype(F32))


U32 = jnp.uint32


def _slab_rows(D):
    return D // 2 // LANES


def _slab_pitch(D):
    pitch = -(-(_slab_rows(D) + 1) // SUBLANES) * SUBLANES
    return pitch + SUBLANES if pitch % 32 == 0 else pitch


def _slab_load(ref, n_tok, D):
    pitch = _slab_pitch(D)
    words = jnp.concatenate([ref[pl.ds(c, n_tok, stride=pitch), :] for c in range(_slab_rows(D))], axis=1)
    hi = lax.bitcast_convert_type(words & jnp.uint32(0xFFFF0000), F32)
    lo = lax.bitcast_convert_type(words << 16, F32)
    return jnp.concatenate([hi, lo], axis=1)


def _slab_store(ref, val, pad=True):
    n_tok, D = val.shape
    pitch = _slab_pitch(D)
    nrow = _slab_rows(D)
    bits = lax.bitcast_convert_type(val.astype(BF16).astype(F32), U32)
    words = bits[:, :D // 2] | (bits[:, D // 2:] >> 16)
    for c in range(nrow):
        ref[pl.ds(c, n_tok, stride=pitch), :] = words[:, c * LANES:(c + 1) * LANES]
    if pad:
        for c in range(nrow, pitch):
            ref[pl.ds(c, n_tok, stride=pitch), :] = jnp.zeros((n_tok, LANES), U32)


def _ln_body(*refs, tm, slabs, weighted, alpha, router, outs):
    n_in = len(slabs)
    pos = n_in
    if weighted:
        aw = refs[pos][...]
        pos += 1
    g_ref, b_ref = refs[pos], refs[pos + 1]
    pos += 2
    D = g_ref.shape[1]
    vals = [_slab_load(r, tm, D) if sl else r[...].astype(F32) for r, sl in zip(refs[:n_in], slabs)]
    if router:
        wr_ref, br_ref = refs[pos], refs[pos + 1]
        pos += 2

    z = alpha * vals[0]
    for k, a in enumerate(vals[1:]):
        z = z + (a * aw[:, k:k + 1] if weighted else a)
    mu = jnp.mean(z, axis=1, keepdims=True)
    zc = z - mu
    var = jnp.mean(zc * zc, axis=1, keepdims=True)
    y = zc * lax.rsqrt(var + NORM_EPS) * g_ref[...] + b_ref[...]
    for kind in outs:
        if kind == "slab":
            _slab_store(refs[pos], y)
        else:
            refs[pos][...] = y.astype(refs[pos].dtype)
        pos += 1

    if router:
        ei_ref, ew_ref = refs[pos], refs[pos + 1]
        ng, epg = N_EXPERT_GROUPS, EXPERTS_PER_GROUP
        lt = lax.dot_general(wr_ref[...], y, (((1,), (1,)), ((), ())),
                             precision=lax.Precision.HIGHEST, preferred_element_type=F32) + br_ref[...]
        ex = jnp.exp(lt - jnp.max(lt, axis=0, keepdims=True))
        p = ex / jnp.sum(ex, axis=0, keepdims=True)
        p0, p1, p2, p3 = [p[j * ng:(j + 1) * ng] for j in range(epg)]
        a, b = jnp.maximum(p0, p1), jnp.minimum(p0, p1)
        c, d = jnp.maximum(p2, p3), jnp.minimum(p2, p3)
        score = jnp.maximum(a, c) + jnp.maximum(jnp.minimum(a, c), jnp.maximum(b, d))
        gi = lax.broadcasted_iota(jnp.int32, score.shape, 0)
        best = jnp.min(jnp.where(score == jnp.max(score, axis=0, keepdims=True), gi, ng),
                       axis=0, keepdims=True)
        sel = gi == best
        v = [jnp.sum(jnp.where(sel, pj, 0.0), axis=0, keepdims=True) for pj in (p0, p1, p2, p3)]

        def first_max(vals):
            top = jnp.maximum(jnp.maximum(vals[0], vals[1]), jnp.maximum(vals[2], vals[3]))
            idx = jnp.where(vals[0] == top, 0, jnp.where(vals[1] == top, 1, jnp.where(vals[2] == top, 2, 3)))
            return top, idx

        v1, j1 = first_max(v)
        v2, j2 = first_max([jnp.where(j1 == j, -1.0, v[j]) for j in range(epg)])
        tot = v1 + v2
        ei_ref[...] = jnp.concatenate([best * epg + j1, best * epg + j2], axis=0).astype(jnp.int32)
        eye = (lax.broadcasted_iota(jnp.int32, (tm, tm), 0) == lax.broadcasted_iota(jnp.int32, (tm, tm), 1))
        cols = [jnp.sum(jnp.where(eye, jnp.broadcast_to(w, (tm, tm)), 0.0), axis=1, keepdims=True)
                for w in (v1 / tot, v2 / tot)]
        lane = lax.broadcasted_iota(jnp.int32, (tm, LANES), 1)
        ew_ref[...] = jnp.where(lane == 0, cols[0], jnp.where(lane == 1, cols[1], 0.0))


def _ln(T, D, inputs, g, b, alpha, outs, add_w=None, router=None, tm=256):
    tm = min(tm, T)
    pitch = _slab_pitch(D)

    def spec(is_slab, off):
        if is_slab:
            return pl.BlockSpec((tm * pitch, LANES), lambda i: (off + i, 0))
        return pl.BlockSpec((tm, D), lambda i: (off + i, 0))

    in_specs = [spec(sl, off) for _, sl, off in inputs]
    args = [a for a, _, _ in inputs]
    if add_w is not None:
        in_specs.append(pl.BlockSpec((tm, LANES), lambda i: (i, 0)))
        args.append(add_w)
    in_specs += [pl.BlockSpec((1, D), lambda i: (0, 0))] * 2
    args += [g.reshape(1, D).astype(F32), b.reshape(1, D).astype(F32)]
    out_shape, out_specs = [], []
    for kind in outs:
        if kind == "slab":
            out_shape.append(jax.ShapeDtypeStruct((T * pitch, LANES), U32))
        else:
            out_shape.append(jax.ShapeDtypeStruct((T, D), F32 if kind == "f32" else BF16))
        out_specs.append(spec(kind == "slab", 0))
    if router is not None:
        wr_t, br = router
        E = wr_t.shape[0]
        in_specs += [pl.BlockSpec((E, D), lambda i: (0, 0)), pl.BlockSpec((E, 1), lambda i: (0, 0))]
        args += [wr_t, br]
        out_shape += [jax.ShapeDtypeStruct((2, T), jnp.int32), jax.ShapeDtypeStruct((T, LANES), F32)]
        out_specs += [pl.BlockSpec((2, tm), lambda i: (0, i)), pl.BlockSpec((tm, LANES), lambda i: (i, 0))]
    return pl.pallas_call(
        functools.partial(_ln_body, tm=tm, slabs=tuple(sl for _, sl, _ in inputs),
                          weighted=add_w is not None, alpha=alpha,
                          router=router is not None, outs=tuple(outs)),
        grid=(T // tm,),
        in_specs=in_specs, out_specs=out_specs, out_shape=out_shape,
        compiler_params=_cparams("parallel"),
        name="deepnorm_ln_router" if router is not None else "deepnorm_ln",
    )(*args)


FFN_SUB = 256
FFN_VISIT_SUBS = 3
FFN_VMEM_LIMIT = 60 * 1024 * 1024


def _ffn_body(ve_ref, vrow_ref, vcnt_ref, nv_ref, ord_ref, tok_ref, h_hbm, hflat_hbm, wg_ref, wu_ref, wd_ref,
              y_hbm, gin_ref, gout_ref, xb_ref, acc_ref, wgb_ref, wub_ref, wdb_ref, pend_ref,
              sem_in, sem_out, *, D):
    v, j = pl.program_id(0), pl.program_id(1)
    n_fc = pl.num_programs(1)
    last_j = n_fc - 1
    pitch = _slab_pitch(D)
    n_data = _slab_rows(D)
    sub = FFN_SUB
    used = v < nv_ref[0]
    cnt, row0 = vcnt_ref[v], vrow_ref[v]

    def slab_row(ref, r):
        return ref.at[pl.ds(pl.multiple_of(r * pitch, SUBLANES), n_data)]

    def wait_rows(n, row_copy, rows_per_copy, vmem_ref, sem):
        @pl.when(n == sub)
        def _():
            whole = pl.ds(0, sub * rows_per_copy)
            pltpu.make_async_copy(hflat_hbm.at[whole], vmem_ref.at[whole], sem).wait()

        @pl.when(n < sub)
        def _():
            def body(r, carry):
                row_copy(r).wait()
                return carry
            lax.fori_loop(0, n, body, 0)

    def start_fetch(first_row, n):
        def body(r, carry):
            src = h_hbm.at[tok_ref[first_row + r], pl.ds(0, n_data)]
            pltpu.make_async_copy(src, slab_row(gin_ref, r), sem_in).start(priority=1)
            return carry
        lax.fori_loop(0, n, body, 0)

    def wait_fetch(n):
        wait_rows(n, lambda r: pltpu.make_async_copy(h_hbm.at[0, pl.ds(0, n_data)], slab_row(gin_ref, r), sem_in),
                  n_data, gin_ref, sem_in)

    def out_row(r):
        return gout_ref.at[pl.ds(pl.multiple_of(r * pitch, SUBLANES), pitch)]

    def start_send(first_row, n):
        def body(r, carry):
            pltpu.make_async_copy(out_row(r), y_hbm.at[ord_ref[first_row + r]], sem_out).start(priority=1)
            return carry
        lax.fori_loop(0, n, body, 0)

    def wait_sends():
        wait_rows(pend_ref[0], lambda r: pltpu.make_async_copy(out_row(r), y_hbm.at[0], sem_out),
                  pitch, gout_ref, sem_out)
        pend_ref[0] = 0

    @pl.when((v == 0) & (j == 0))
    def _():
        gin_ref[...] = jnp.zeros_like(gin_ref)
        gout_ref[...] = jnp.zeros_like(gout_ref)
        xb_ref[...] = jnp.zeros_like(xb_ref)
        pend_ref[0] = 0
        start_fetch(row0, jnp.minimum(cnt, sub))

    @pl.when(used)
    def _():
        wgb_ref[...] = wg_ref[...].astype(BF16)
        wub_ref[...] = wu_ref[...].astype(BF16)
        wdb_ref[...] = wd_ref[...].astype(BF16)

    for s in range(FFN_VISIT_SUBS):
        n_rows = jnp.minimum(cnt - s * sub, sub)
        active = used & (s * sub < cnt)
        rows = slice(s * sub, (s + 1) * sub)

        @pl.when(active & (j == 0))
        def _():
            wait_fetch(n_rows)
            xb_ref[rows, :] = _slab_load(gin_ref, sub, D).astype(BF16)
            if s + 1 < FFN_VISIT_SUBS:
                @pl.when((s + 1) * sub < cnt)
                def _():
                    start_fetch(row0 + (s + 1) * sub, jnp.minimum(cnt - (s + 1) * sub, sub))

        @pl.when(active)
        def _():
            x = xb_ref[rows, :]
            g = jnp.dot(x, wgb_ref[...], preferred_element_type=F32)
            u = jnp.dot(x, wub_ref[...], preferred_element_type=F32)
            hmid = g * (1.0 / (1.0 + jnp.exp(-g))) * u
            part = jnp.dot(hmid.astype(BF16), wdb_ref[...], preferred_element_type=F32)

            @pl.when(j == 0)
            def _():
                acc_ref[rows, :] = part

            @pl.when(j > 0)
            def _():
                acc_ref[rows, :] += part

        @pl.when(active & (j == last_j))
        def _():
            wait_sends()
            _slab_store(gout_ref, acc_ref[rows, :], pad=False)
            start_send(row0 + s * sub, n_rows)
            pend_ref[0] = n_rows

    @pl.when((j == jnp.minimum(1, last_j)) & (v + 1 < nv_ref[0]))
    def _():
        start_fetch(vrow_ref[v + 1], jnp.minimum(vcnt_ref[v + 1], sub))

    @pl.when((v == pl.num_programs(0) - 1) & (j == last_j))
    def _():
        wait_sends()


def _moe(h_slab, eidx, w_gate, w_up, w_down, layer, T, D):
    E, F = w_gate.shape[1], w_gate.shape[-1]
    A = 2 * T
    pitch = _slab_pitch(D)
    fc = 256 if F % 256 == 0 else F
    visit_rows = FFN_VISIT_SUBS * FFN_SUB
    n_visits = E + A // visit_rows

    e_flat = eidx.reshape(A)
    order = jnp.argsort(e_flat, stable=True).astype(jnp.int32)
    experts = jnp.arange(E, dtype=jnp.int32)
    counts = jnp.sum((e_flat[:, None] == experts[None, :]).astype(jnp.int32), axis=0)
    start = jnp.cumsum(counts) - counts
    visits_e = (counts + visit_rows - 1) // visit_rows
    visit_end = jnp.cumsum(visits_e)
    n_used = visit_end[-1:].astype(jnp.int32)
    vid = jnp.minimum(jnp.arange(n_visits, dtype=jnp.int32), n_used[0] - 1)
    visit_expert = jnp.sum((vid[:, None] >= visit_end[None, :]).astype(jnp.int32), axis=1).astype(jnp.int32)
    k_in_expert = vid - (visit_end - visits_e)[visit_expert]
    visit_row0 = (start[visit_expert] + k_in_expert * visit_rows).astype(jnp.int32)
    visit_cnt = jnp.minimum(counts[visit_expert] - k_in_expert * visit_rows, visit_rows).astype(jnp.int32)

    n_fc = F // fc

    def chunk(v, j, nv):
        return jnp.where(v < nv[0], j, n_fc - 1)

    def w_map(v, j, ve, vrow, vcnt, nv, ordr, tok):
        return (layer, ve[v], 0, chunk(v, j, nv))

    out = pl.pallas_call(
        functools.partial(_ffn_body, D=D),
        grid_spec=pltpu.PrefetchScalarGridSpec(
            num_scalar_prefetch=6, grid=(n_visits, n_fc),
            in_specs=[pl.BlockSpec(memory_space=pl.ANY),
                      pl.BlockSpec(memory_space=pl.ANY),
                      pl.BlockSpec((None, None, D, fc), w_map),
                      pl.BlockSpec((None, None, D, fc), w_map),
                      pl.BlockSpec((None, None, fc, D),
                                   lambda v, j, ve, vrow, vcnt, nv, ordr, tok: (layer, ve[v], chunk(v, j, nv), 0))],
            out_specs=pl.BlockSpec(memory_space=pl.ANY),
            scratch_shapes=[pltpu.VMEM((FFN_SUB * pitch, LANES), U32), pltpu.VMEM((FFN_SUB * pitch, LANES), U32),
                            pltpu.VMEM((visit_rows, D), BF16), pltpu.VMEM((visit_rows, D), F32),
                            pltpu.VMEM((D, fc), BF16), pltpu.VMEM((D, fc), BF16), pltpu.VMEM((fc, D), BF16),
                            pltpu.SMEM((1,), jnp.int32), pltpu.SemaphoreType.DMA(()),
                            pltpu.SemaphoreType.DMA(())]),
        out_shape=jax.ShapeDtypeStruct((A, pitch, LANES), U32),
        compiler_params=_cparams("arbitrary", "arbitrary", vmem_limit=FFN_VMEM_LIMIT),
        name="expert_ffn",
    )(visit_expert, visit_row0, visit_cnt, n_used, order, jnp.where(order >= T, order - T, order),
      h_slab.reshape(T, pitch, LANES), h_slab, w_gate, w_up, w_down)
    return out.reshape(A * pitch, LANES)


def _proj_heads_body(x_ref, w_ref, ca_ref, sb_ref, sc_ref, o_ref, acc_ref, *, d, hb, rope):
    acc = jnp.dot(x_ref[...], w_ref[...].astype(BF16), preferred_element_type=F32)
    tm = x_ref.shape[0]
    n = tm // d
    half = ROPE_DIM // 2
    if d > 1:
        for hh in range(hb):
            acc_ref[hh * tm:(hh + 1) * tm, :] = acc[:, hh * HEAD_DIM:(hh + 1) * HEAD_DIM]
    for r in range(d):
        if rope:
            rows = pl.ds(r, n, stride=d) if d > 1 else pl.ds(0, n)
            ca, sb, sc = ca_ref[rows, :], sb_ref[rows, :], sc_ref[rows, :]
        for hh in range(hb):
            if d > 1:
                val = acc_ref[pl.ds(hh * tm + r, n, stride=d), :]
            else:
                val = acc[:, hh * HEAD_DIM:(hh + 1) * HEAD_DIM]
            if rope:
                val = (val * ca + pltpu.roll(val, half, 1) * sb
                       + pltpu.roll(val, HEAD_DIM - half, 1) * sc)
            o_ref[hh, r] = val.astype(o_ref.dtype)


def _proj_heads(x, w, layer, col0, n_heads, d, B, S, tables, rope, tm=1024, hb=4):
    M, K = x.shape
    tm = min(tm, S)
    tn = hb * HEAD_DIM
    assert S % tm == 0 and n_heads % hb == 0 and col0 % tn == 0 and (tm // d) % 16 == 0
    mt = S // tm
    c0 = col0 // tn
    tab_spec = pl.BlockSpec((tm, HEAD_DIM), lambda i, j: (i % mt, 0))
    return pl.pallas_call(
        functools.partial(_proj_heads_body, d=d, hb=hb, rope=rope),
        grid=(M // tm, n_heads // hb),
        in_specs=[pl.BlockSpec((tm, K), lambda i, j: (i, 0)),
                  pl.BlockSpec((None, K, tn), lambda i, j: (layer, 0, c0 + j)),
                  tab_spec, tab_spec, tab_spec],
        out_specs=pl.BlockSpec((None, hb, d, tm // d, HEAD_DIM), lambda i, j: (i // mt, j, 0, i % mt, 0)),
        out_shape=jax.ShapeDtypeStruct((B, n_heads, d, S // d, HEAD_DIM), BF16),
        scratch_shapes=[pltpu.VMEM((hb * tm, HEAD_DIM), F32)],
        compiler_params=_cparams("parallel", "parallel"),
        name="attn_proj_d%d" % d,
    )(x, w, *tables)


def _rope_tables(S, scale):
    half = ROPE_DIM // 2
    inv_freq = ROPE_THETA ** (-jnp.arange(half, dtype=F32) / half)
    ang = jnp.arange(S).astype(F32)[:, None] * inv_freq[None, :]
    cos, sin = jnp.cos(ang), jnp.sin(ang)
    zeros = jnp.zeros((S, HEAD_DIM - ROPE_DIM), F32)
    ca = jnp.concatenate([cos, cos, jnp.ones_like(zeros)], axis=1)
    sb = jnp.concatenate([jnp.zeros_like(sin), sin, zeros], axis=1)
    sc = jnp.concatenate([-sin, jnp.zeros_like(sin), zeros], axis=1)
    return ca * scale, sb * scale, sc * scale


def _attn_body(q_ref, kc_ref, kp_ref, vc_ref, vp_ref, o_ref, lse_ref, kbuf, vbuf):
    SB, LB, _ = q_ref.shape
    blk = ATT_BLOCK
    n = pl.program_id(1)
    kbuf[:, :blk] = kp_ref[...]
    kbuf[:, blk:] = kc_ref[...]
    vbuf[:, :blk] = vp_ref[...]
    vbuf[:, blk:] = vc_ref[...]

    row = lax.broadcasted_iota(jnp.int32, (blk, 2 * blk), 0)
    col = lax.broadcasted_iota(jnp.int32, (blk, 2 * blk), 1)
    band = (col >= row) & (col <= row + blk)
    band_first = band & ((col >= blk) | (n > 0))
    eye = (lax.broadcasted_iota(jnp.int32, (blk, blk), 0) == lax.broadcasted_iota(jnp.int32, (blk, blk), 1))

    for sb in range(SB):
        for qi in range(LB // blk):
            q = q_ref[sb, qi * blk:(qi + 1) * blk, :]
            kk = kbuf[sb, qi * blk:(qi + 2) * blk, :]
            vv = vbuf[sb, qi * blk:(qi + 2) * blk, :]
            s = lax.dot_general(q, kk, (((1,), (1,)), ((), ())), preferred_element_type=F32)
            s = jnp.where(band_first if qi == 0 else band, s, NEG)
            m = jnp.max(s, axis=1, keepdims=True)
            p = jnp.exp(s - m)
            l = jnp.sum(p, axis=1, keepdims=True)
            o = jnp.dot(p.astype(BF16), vv, preferred_element_type=F32) / l
            o_ref[sb, qi * blk:(qi + 1) * blk, :] = o.astype(o_ref.dtype)
            lse = m + jnp.log(l)
            lse_ref[sb, qi:qi + 1, :] = jnp.sum(
                jnp.where(eye, jnp.broadcast_to(lse, (blk, blk)), 0.0), axis=0, keepdims=True)


def _band_attention(q, k, v, rows_per_step=2048):
    NS, L, Dh = q.shape
    blk = ATT_BLOCK
    LB = min(L, rows_per_step)
    SB = max(1, rows_per_step // LB)
    assert L % LB == 0 and NS % SB == 0 and LB % blk == 0
    nlb = LB // blk
    cur = pl.BlockSpec((SB, LB, Dh), lambda s, n: (s, n, 0))
    prev = pl.BlockSpec((SB, blk, Dh), lambda s, n: (s, jnp.maximum(n * nlb - 1, 0), 0))
    return pl.pallas_call(
        _attn_body,
        grid=(NS // SB, L // LB),
        in_specs=[cur, cur, prev, cur, prev],
        out_specs=[cur, pl.BlockSpec((SB, nlb, blk), lambda s, n: (s, n, 0))],
        out_shape=[jax.ShapeDtypeStruct((NS, L, Dh), BF16),
                   jax.ShapeDtypeStruct((NS, L // blk, blk), F32)],
        scratch_shapes=[pltpu.VMEM((SB, LB + blk, Dh), BF16), pltpu.VMEM((SB, LB + blk, Dh), BF16)],
        compiler_params=_cparams("parallel", "arbitrary"),
        name="band_attention",
    )(q, k, k, v, v)


def _combine_body(*refs, dils):
    G = len(dils)
    o_refs, l_refs = refs[:G], refs[G:2 * G]
    out_ref, acc_ref, wcol_ref = refs[2 * G:]
    S = out_ref.shape[0]
    blk = LANES
    eye = (lax.broadcasted_iota(jnp.int32, (blk, blk), 0) == lax.broadcasted_iota(jnp.int32, (blk, blk), 1))

    lses = [l[...] for l in l_refs]
    mx = functools.reduce(jnp.maximum, lses)
    es = [jnp.exp(l - mx) for l in lses]
    tot = functools.reduce(lambda a, b: a + b, es)
    for g in range(G):
        mix = es[g] / tot
        for tb in range(S // blk):
            seg = jnp.broadcast_to(mix[:, tb * blk:(tb + 1) * blk], (blk, blk))
            col = jnp.sum(jnp.where(eye, seg, 0.0), axis=1, keepdims=True)
            wcol_ref[g * S + tb * blk:g * S + (tb + 1) * blk, :] = jnp.broadcast_to(col, (blk, blk))
    for g, d in enumerate(dils):
        n = S // d
        for r in range(d):
            rows = pl.ds(r, n, stride=d) if d > 1 else pl.ds(0, n)
            wrows = pl.ds(g * S + r, n, stride=d) if d > 1 else pl.ds(g * S, n)
            contrib = o_refs[g][r].astype(F32) * wcol_ref[wrows, :]
            if g == 0:
                acc_ref[rows, :] = contrib
            else:
                acc_ref[rows, :] += contrib
    out_ref[...] = acc_ref[...].astype(out_ref.dtype)


def _combine(os_, lses, B, S, n_heads):
    dils = tuple(d for _, d in ATT_GROUPS)
    in_specs = [pl.BlockSpec((None, None, d, S // d, HEAD_DIM), lambda b, h: (b, h, 0, 0, 0)) for d in dils]
    in_specs += [pl.BlockSpec((None, None, 1, S), lambda b, h: (b, h, 0, 0))] * len(dils)
    return pl.pallas_call(
        functools.partial(_combine_body, dils=dils),
        grid=(B, n_heads),
        in_specs=in_specs,
        out_specs=pl.BlockSpec((None, S, HEAD_DIM), lambda b, h: (b, 0, h)),
        out_shape=jax.ShapeDtypeStruct((B, S, n_heads * HEAD_DIM), BF16),
        scratch_shapes=[pltpu.VMEM((S, HEAD_DIM), F32), pltpu.VMEM((len(dils) * S, LANES), F32)],
        compiler_params=_cparams("parallel", "parallel"),
        name="group_combine",
    )(*os_, *lses)


def kernel(x, mlstm_w_in, mlstm_b_gate, mlstm_norm_g, mlstm_w_out, att_w_q, att_w_kv_shared, att_w_o,
           router_w, router_b, expert_w_gate, expert_w_up, expert_w_down, ln_g, ln_b):
    B, S, D = x.shape
    T = B * S
    depth = ln_g.shape[0]
    n_a = mlstm_w_in.shape[0]
    alpha = (2.0 * depth) ** 0.25

    Hm = mlstm_b_gate.shape[-1] // 2
    dv = mlstm_w_out.shape[1] // Hm
    dk = (mlstm_w_in.shape[-1] - 2 * Hm - 2 * Hm * dv) // (2 * Hm)
    n_main = 2 * Hm * dk + 2 * Hm * dv

    AH = att_w_o.shape[1] // HEAD_DIM
    G = len(ATT_GROUPS)
    AW = G * AH * HEAD_DIM
    E = router_w.shape[1]
    assert E == N_EXPERT_GROUPS * EXPERTS_PER_GROUP

    perm = (jnp.arange(E) % N_EXPERT_GROUPS) * EXPERTS_PER_GROUP + jnp.arange(E) // N_EXPERT_GROUPS
    router = (router_w.T[perm].astype(F32), router_b[perm].reshape(E, 1).astype(F32))

    h = x.reshape(T, D)
    h_b = h.astype(BF16)
    kv = None
    for layer in range(depth):
        if layer < n_a:
            w_in_t = jnp.swapaxes(mlstm_w_in, 1, 2)
            proj = _mm(h_b, w_in_t, layer, 0, n_main, BF16, w_transposed=True)
            gates_t = _mlstm_gates(h_b, w_in_t[layer, n_main:], mlstm_b_gate[layer])
            hm = _mlstm(proj.reshape(B, S, n_main), gates_t, mlstm_norm_g[layer], B, S, Hm, dk, dv)
            mix = _mm(hm.reshape(T, Hm * dv), mlstm_w_out, layer, 0, D, BF16)
        else:
            bl = layer - n_a
            k_tabs = _rope_tables(S, 1.0)
            q_tabs = _rope_tables(S, HEAD_DIM ** -0.5)
            w_kv = att_w_kv_shared.reshape(1, D, 2 * AW)
            os_, lses = [], []
            for g, (window, d) in enumerate(ATT_GROUPS):
                assert window // d == ATT_BLOCK and (S // d) % ATT_BLOCK == 0
                if kv is None or len(kv) <= g:
                    kv = (kv or []) + [(
                        _proj_heads(h_b, w_kv, 0, g * AH * HEAD_DIM, AH, d, B, S, k_tabs, True),
                        _proj_heads(h_b, w_kv, 0, AW + g * AH * HEAD_DIM, AH, d, B, S, k_tabs, False))]
                kg, vg = kv[g]
                qg = _proj_heads(h_b, att_w_q, bl, g * AH * HEAD_DIM, AH, d, B, S, q_tabs, True)
                L = S // d
                ns = B * AH * d
                o, lse = _band_attention(qg.reshape(ns, L, HEAD_DIM), kg.reshape(ns, L, HEAD_DIM),
                                         vg.reshape(ns, L, HEAD_DIM))
                os_.append(o.reshape(B, AH, d, L, HEAD_DIM))
                lses.append(lse.reshape(B, AH, d, L).transpose(0, 1, 3, 2).reshape(B, AH, 1, S))
            ob = _combine(os_, lses, B, S, AH)
            mix = _mm(ob.reshape(T, AH * HEAD_DIM), att_w_o, bl, 0, D, BF16)

        tm_ln = min(256, T)
        h1, h1_slab, eidx, ew = _ln(T, D, [(h, False, 0), (mix, False, 0)], ln_g[layer, 0], ln_b[layer, 0],
                                    alpha, ("f32", "slab"), router=router, tm=tm_ln)
        y2 = _moe(h1_slab, eidx, expert_w_gate, expert_w_up, expert_w_down, layer, T, D)
        last = layer == depth - 1
        outs = _ln(T, D, [(h1, False, 0), (y2, True, 0), (y2, True, T // tm_ln)],
                   ln_g[layer, 1], ln_b[layer, 1], alpha, ("f32",) if last else ("f32", "bf16"),
                   add_w=ew, tm=tm_ln)
        h = outs[0]
        h_b = None if last else outs[1]
    return h.reshape(B, S, D)
```

```python
import functools
import math

import jax
import jax.numpy as jnp
from jax import lax
from jax.experimental import pallas as pl
from jax.experimental.pallas import tpu as pltpu

F32 = jnp.float32
BF16 = jnp.bfloat16

HEAD_DIM = 128
ATT_GROUPS = ((128, 1), (512, 4), (2048, 16))
ATT_BLOCK = 128
ROPE_DIM = HEAD_DIM // 4
ROPE_THETA = 500000.0
N_EXPERT_GROUPS = 8
EXPERTS_PER_GROUP = 4
NORM_EPS = 1e-5
MLSTM_CHUNK = 256
LANES = 128
SUBLANES = 8
VMEM_LIMIT = 56 * 1024 * 1024
NEG = -1e30


def _cparams(*sem):
    return pltpu.CompilerParams(dimension_semantics=sem, vmem_limit_bytes=VMEM_LIMIT)


def _mm_body(x_ref, w_ref, o_ref, *, w_transposed):
    contract = (((1,), (1 if w_transposed else 0,)), ((), ()))
    o_ref[...] = lax.dot_general(x_ref[...], w_ref[...].astype(BF16), contract,
                                 preferred_element_type=F32).astype(o_ref.dtype)


def _mm(x, w, layer, col0, ncols, out_dtype, w_transposed=False, tm=1024, tn=512):
    M, K = x.shape
    tm = min(tm, M)
    assert M % tm == 0 and ncols % tn == 0 and col0 % tn == 0
    c0 = col0 // tn
    if w_transposed:
        w_spec = pl.BlockSpec((None, tn, K), lambda i, j: (layer, c0 + j, 0))
    else:
        w_spec = pl.BlockSpec((None, K, tn), lambda i, j: (layer, 0, c0 + j))
    return pl.pallas_call(
        functools.partial(_mm_body, w_transposed=w_transposed),
        grid=(M // tm, ncols // tn),
        in_specs=[pl.BlockSpec((tm, K), lambda i, j: (i, 0)), w_spec],
        out_specs=pl.BlockSpec((tm, tn), lambda i, j: (i, j)),
        out_shape=jax.ShapeDtypeStruct((M, ncols), out_dtype),
        compiler_params=_cparams("parallel", "parallel"),
        name="dense_proj",
    )(x, w)


def _gates_body(wt_ref, x_ref, b_ref, o_ref, *, n_heads):
    g = lax.dot_general(wt_ref[...].astype(BF16), x_ref[...], (((1,), (1,)), ((), ())),
                        preferred_element_type=F32)
    g = g + b_ref[...]
    log_sig = jnp.minimum(g, 0.0) - jnp.log1p(jnp.exp(-jnp.abs(g)))
    row = lax.broadcasted_iota(jnp.int32, g.shape, 0)
    o_ref[...] = jnp.where(row >= n_heads, log_sig, g)


def _mlstm_gates(x, w_gates_t, b_gate, tm=512):
    T, D = x.shape
    H2 = w_gates_t.shape[0]
    tm = min(tm, T)
    return pl.pallas_call(
        functools.partial(_gates_body, n_heads=H2 // 2),
        grid=(T // tm,),
        in_specs=[pl.BlockSpec((H2, D), lambda i: (0, 0)),
                  pl.BlockSpec((tm, D), lambda i: (i, 0)),
                  pl.BlockSpec((H2, 1), lambda i: (0, 0))],
        out_specs=pl.BlockSpec((H2, tm), lambda i: (0, i)),
        out_shape=jax.ShapeDtypeStruct((H2, T), F32),
        compiler_params=_cparams("parallel"),
        name="mlstm_gates",
    )(w_gates_t, x, b_gate.reshape(H2, 1).astype(F32))


def _mlstm_body(q_ref, k_ref, v_ref, op_ref, li_ref, lf_ref, g_ref, out_ref,
                c_ref, n_ref, m_ref, *, scale):
    L = q_ref.shape[0]

    @pl.when(pl.program_id(2) == 0)
    def _():
        c_ref[...] = jnp.zeros_like(c_ref)
        n_ref[...] = jnp.zeros_like(n_ref)
        m_ref[...] = jnp.zeros_like(m_ref)

    q = q_ref[...]
    k = k_ref[...]
    v = v_ref[...]
    li = li_ref[...]
    lf = lf_ref[...]
    m_prev = m_ref[...]

    row = lax.broadcasted_iota(jnp.int32, (L, L), 0)
    col = lax.broadcasted_iota(jnp.int32, (L, L), 1)
    causal = col <= row
    eye = col == row

    def to_col(r):
        return jnp.sum(jnp.where(eye, jnp.broadcast_to(r, (L, L)), 0.0), axis=1, keepdims=True)

    b_col = jnp.sum(jnp.where(causal, jnp.broadcast_to(lf, (L, L)), 0.0), axis=1, keepdims=True)
    b_row = jnp.sum(jnp.where(eye, jnp.broadcast_to(b_col, (L, L)), 0.0), axis=0, keepdims=True)
    a_row = li - b_row
    a_b = jnp.broadcast_to(a_row, (L, L))
    mloc = jnp.maximum(m_prev, jnp.max(jnp.where(causal, a_b, NEG), axis=1, keepdims=True))
    dmat = jnp.exp(jnp.where(causal, a_b - mloc, NEG))
    inter = jnp.exp(m_prev - mloc)

    s = lax.dot_general(q, k, (((1,), (1,)), ((), ())), preferred_element_type=F32) * scale * dmat
    qc = jnp.dot(q, c_ref[...].astype(BF16), preferred_element_type=F32) * scale
    num = inter * qc + jnp.dot(s.astype(BF16), v, preferred_element_type=F32)
    qn = jnp.sum(q.astype(F32) * n_ref[...], axis=1, keepdims=True) * scale
    den = inter * qn + jnp.sum(s, axis=1, keepdims=True)
    m_t = b_col + mloc
    h = num / jnp.maximum(jnp.abs(den), jnp.exp(-m_t))

    m_last = jnp.maximum(m_prev, jnp.max(a_row, axis=1, keepdims=True))
    b_last = jnp.sum(lf, axis=1, keepdims=True)
    decay = jnp.exp(m_prev - m_last)
    w_col = to_col(jnp.exp(a_row - m_last))
    wk = w_col * k.astype(F32)
    c_ref[...] = decay * c_ref[...] + lax.dot_general(
        wk.astype(BF16), v, (((0,), (0,)), ((), ())), preferred_element_type=F32)
    n_ref[...] = decay * n_ref[...] + jnp.sum(wk, axis=0, keepdims=True)
    m_ref[...] = b_last + m_last

    hn = h * lax.rsqrt(jnp.mean(h * h, axis=1, keepdims=True) + NORM_EPS)
    gate = 1.0 / (1.0 + jnp.exp(-op_ref[...].astype(F32)))
    out_ref[...] = (hn * g_ref[...] * gate).astype(out_ref.dtype)


def _mlstm(proj, gates_t, norm_g, B, S, H, dk, dv):
    L = min(MLSTM_CHUNK, S)
    nc = S // L
    assert S % L == 0 and (2 * H * dk) % dv == 0
    v0 = (2 * H * dk) // dv
    gates3 = gates_t.reshape(2 * H, 1, B * S)
    return pl.pallas_call(
        functools.partial(_mlstm_body, scale=dk ** -0.5),
        grid=(B, H, nc),
        in_specs=[pl.BlockSpec((None, L, dk), lambda b, h, c: (b, c, h)),
                  pl.BlockSpec((None, L, dk), lambda b, h, c: (b, c, H + h)),
                  pl.BlockSpec((None, L, dv), lambda b, h, c: (b, c, v0 + h)),
                  pl.BlockSpec((None, L, dv), lambda b, h, c: (b, c, v0 + H + h)),
                  pl.BlockSpec((None, 1, L), lambda b, h, c: (h, 0, b * nc + c)),
                  pl.BlockSpec((None, 1, L), lambda b, h, c: (H + h, 0, b * nc + c)),
                  pl.BlockSpec((1, dv), lambda b, h, c: (0, h))],
        out_specs=pl.BlockSpec((None, L, dv), lambda b, h, c: (b, c, h)),
        out_shape=jax.ShapeDtypeStruct((B, S, H * dv), BF16),
        scratch_shapes=[pltpu.VMEM((dk, dv), F32), pltpu.VMEM((1, dk), F32), pltpu.VMEM((1, 1), F32)],
        compiler_params=_cparams("parallel", "parallel", "arbitrary"),
        name="mlstm_chunks",
    )(proj, proj, proj, proj, gates3, gates3, norm_g.reshape(1, H * dv).astype(F32))


def _slab_pitch(D):
    return D // LANES + SUBLANES


def _slab_load(ref, n_tok, D):
    pitch = _slab_pitch(D)
    return jnp.concatenate([ref[pl.ds(c, n_tok, stride=pitch), :] for c in range(D // LANES)], axis=1)


def _slab_store(ref, val, pad=True):
    n_tok, D = val.shape
    pitch = _slab_pitch(D)
    nrow = D // LANES
    for c in range(nrow):
        ref[pl.ds(c, n_tok, stride=pitch), :] = val[:, c * LANES:(c + 1) * LANES].astype(ref.dtype)
    if pad:
        for c in range(nrow, pitch):
            ref[pl.ds(c, n_tok, stride=pitch), :] = jnp.zeros((n_tok, LANES), ref.dtype)


def _ln_body(*refs, tm, slabs, weighted, alpha, router, outs):
    n_in = len(slabs)
    pos = n_in
    if weighted:
        aw = refs[pos][...]
        pos += 1
    g_ref, b_ref = refs[pos], refs[pos + 1]
    pos += 2
    D = g_ref.shape[1]
    vals = [_slab_load(r, tm, D) if sl else r[...].astype(F32) for r, sl in zip(refs[:n_in], slabs)]
    if router:
        wr_ref, br_ref = refs[pos], refs[pos + 1]
        pos += 2

    z = alpha * vals[0]
    for k, a in enumerate(vals[1:]):
        z = z + (a * aw[:, k:k + 1] if weighted else a)
    mu = jnp.mean(z, axis=1, keepdims=True)
    zc = z - mu
    var = jnp.mean(zc * zc, axis=1, keepdims=True)
    y = zc * lax.rsqrt(var + NORM_EPS) * g_ref[...] + b_ref[...]
    for kind in outs:
        if kind == "slab":
            _slab_store(refs[pos], y)
        else:
            refs[pos][...] = y.astype(refs[pos].dtype)
        pos += 1

    if router:
        ei_ref, ew_ref = refs[pos], refs[pos + 1]
        ng, epg = N_EXPERT_GROUPS, EXPERTS_PER_GROUP
        lt = lax.dot_general(wr_ref[...], y, (((1,), (1,)), ((), ())),
                             precision=lax.Precision.HIGHEST, preferred_element_type=F32) + br_ref[...]
        ex = jnp.exp(lt - jnp.max(lt, axis=0, keepdims=True))
        p = ex / jnp.sum(ex, axis=0, keepdims=True)
        p0, p1, p2, p3 = [p[j * ng:(j + 1) * ng] for j in range(epg)]
        a, b = jnp.maximum(p0, p1), jnp.minimum(p0, p1)
        c, d = jnp.maximum(p2, p3), jnp.minimum(p2, p3)
        score = jnp.maximum(a, c) + jnp.maximum(jnp.minimum(a, c), jnp.maximum(b, d))
        gi = lax.broadcasted_iota(jnp.int32, score.shape, 0)
        best = jnp.min(jnp.where(score == jnp.max(score, axis=0, keepdims=True), gi, ng),
                       axis=0, keepdims=True)
        sel = gi == best
        v = [jnp.sum(jnp.where(sel, pj, 0.0), axis=0, keepdims=True) for pj in (p0, p1, p2, p3)]

        def first_max(vals):
            top = jnp.maximum(jnp.maximum(vals[0], vals[1]), jnp.maximum(vals[2], vals[3]))
            idx = jnp.where(vals[0] == top, 0, jnp.where(vals[1] == top, 1, jnp.where(vals[2] == top, 2, 3)))
            return top, idx

        v1, j1 = first_max(v)
        v2, j2 = first_max([jnp.where(j1 == j, -1.0, v[j]) for j in range(epg)])
        tot = v1 + v2
        ei_ref[...] = jnp.concatenate([best * epg + j1, best * epg + j2], axis=0).astype(jnp.int32)
        eye = (lax.broadcasted_iota(jnp.int32, (tm, tm), 0) == lax.broadcasted_iota(jnp.int32, (tm, tm), 1))
        cols = [jnp.sum(jnp.where(eye, jnp.broadcast_to(w, (tm, tm)), 0.0), axis=1, keepdims=True)
                for w in (v1 / tot, v2 / tot)]
        lane = lax.broadcasted_iota(jnp.int32, (tm, LANES), 1)
        ew_ref[...] = jnp.where(lane == 0, cols[0], jnp.where(lane == 1, cols[1], 0.0))


def _ln(T, D, inputs, g, b, alpha, outs, add_w=None, router=None, tm=256):
    tm = min(tm, T)
    pitch = _slab_pitch(D)

    def spec(is_slab, off):
        if is_slab:
            return pl.BlockSpec((tm * pitch, LANES), lambda i: (off + i, 0))
        return pl.BlockSpec((tm, D), lambda i: (off + i, 0))

    in_specs = [spec(sl, off) for _, sl, off in inputs]
    args = [a for a, _, _ in inputs]
    if add_w is not None:
        in_specs.append(pl.BlockSpec((tm, LANES), lambda i: (i, 0)))
        args.append(add_w)
    in_specs += [pl.BlockSpec((1, D), lambda i: (0, 0))] * 2
    args += [g.reshape(1, D).astype(F32), b.reshape(1, D).astype(F32)]
    out_shape, out_specs = [], []
    for kind in outs:
        if kind == "slab":
            out_shape.append(jax.ShapeDtypeStruct((T * pitch, LANES), F32))
        else:
            out_shape.append(jax.ShapeDtypeStruct((T, D), F32 if kind == "f32" else BF16))
        out_specs.append(spec(kind == "slab", 0))
    if router is not None:
        wr_t, br = router
        E = wr_t.shape[0]
        in_specs += [pl.BlockSpec((E, D), lambda i: (0, 0)), pl.BlockSpec((E, 1), lambda i: (0, 0))]
        args += [wr_t, br]
        out_shape += [jax.ShapeDtypeStruct((2, T), jnp.int32), jax.ShapeDtypeStruct((T, LANES), F32)]
        out_specs += [pl.BlockSpec((2, tm), lambda i: (0, i)), pl.BlockSpec((tm, LANES), lambda i: (i, 0))]
    return pl.pallas_call(
        functools.partial(_ln_body, tm=tm, slabs=tuple(sl for _, sl, _ in inputs),
                          weighted=add_w is not None, alpha=alpha,
                          router=router is not None, outs=tuple(outs)),
        grid=(T // tm,),
        in_specs=in_specs, out_specs=out_specs, out_shape=out_shape,
        compiler_params=_cparams("parallel"),
        name="deepnorm_ln_router" if router is not None else "deepnorm_ln",
    )(*args)


FFN_SUB = 256
FFN_VISIT_SUBS = 3
FFN_VMEM_LIMIT = 62 * 1024 * 1024


def _ffn_body(ve_ref, vrow_ref, vcnt_ref, nv_ref, ord_ref, tok_ref, h_hbm, hflat_hbm, wg_ref, wu_ref, wd_ref,
              y_hbm, gin_ref, gout0_ref, gout1_ref, xb_ref, acc_ref, pend_ref, sem_in, sem_out, *, D):
    v, j = pl.program_id(0), pl.program_id(1)
    n_fc = pl.num_programs(1)
    last_j = n_fc - 1
    pitch = _slab_pitch(D)
    n_data = D // LANES
    sub = FFN_SUB
    used = v < nv_ref[0]
    cnt, row0 = vcnt_ref[v], vrow_ref[v]

    def slab_row(ref, r):
        return ref.at[pl.ds(pl.multiple_of(r * pitch, SUBLANES), n_data)]

    def wait_rows(n, row_copy, rows_per_copy, vmem_ref, sem):
        @pl.when(n == sub)
        def _():
            whole = pl.ds(0, sub * rows_per_copy)
            pltpu.make_async_copy(hflat_hbm.at[whole], vmem_ref.at[whole], sem).wait()

        @pl.when(n < sub)
        def _():
            def body(r, carry):
                row_copy(r).wait()
                return carry
            lax.fori_loop(0, n, body, 0)

    def start_fetch(first_row, n):
        def body(r, carry):
            src = h_hbm.at[tok_ref[first_row + r], pl.ds(0, n_data)]
            pltpu.make_async_copy(src, slab_row(gin_ref, r), sem_in).start(priority=1)
            return carry
        lax.fori_loop(0, n, body, 0)

    def wait_fetch(n):
        wait_rows(n, lambda r: pltpu.make_async_copy(h_hbm.at[0, pl.ds(0, n_data)], slab_row(gin_ref, r), sem_in),
                  n_data, gin_ref, sem_in)

    gouts = (gout0_ref, gout1_ref)

    def out_row(b, r):
        return gouts[b].at[pl.ds(pl.multiple_of(r * pitch, SUBLANES), pitch)]

    def start_send(b, first_row, n):
        def body(r, carry):
            dst = y_hbm.at[ord_ref[first_row + r]]
            pltpu.make_async_copy(out_row(b, r), dst, sem_out.at[b]).start(priority=1)
            return carry
        lax.fori_loop(0, n, body, 0)
        pend_ref[b] = n

    def wait_sends(b):
        wait_rows(pend_ref[b], lambda r: pltpu.make_async_copy(out_row(b, r), y_hbm.at[0], sem_out.at[b]),
                  pitch, gouts[b], sem_out.at[b])
        pend_ref[b] = 0

    @pl.when((v == 0) & (j == 0))
    def _():
        gin_ref[...] = jnp.zeros_like(gin_ref)
        xb_ref[...] = jnp.zeros_like(xb_ref)
        for b in range(2):
            gouts[b][...] = jnp.zeros_like(gouts[b])
            pend_ref[b] = 0
        start_fetch(row0, jnp.minimum(cnt, sub))

    for s in range(FFN_VISIT_SUBS):
        n_rows = jnp.minimum(cnt - s * sub, sub)
        active = used & (s * sub < cnt)
        rows = slice(s * sub, (s + 1) * sub)

        @pl.when(active & (j == 0))
        def _():
            wait_fetch(n_rows)
            xb_ref[rows, :] = _slab_load(gin_ref, sub, D).astype(BF16)
            if s + 1 < FFN_VISIT_SUBS:
                @pl.when((s + 1) * sub < cnt)
                def _():
                    start_fetch(row0 + (s + 1) * sub, jnp.minimum(cnt - (s + 1) * sub, sub))

        @pl.when(active)
        def _():
            x = xb_ref[rows, :]
            g = jnp.dot(x, wg_ref[...].astype(BF16), preferred_element_type=F32)
            u = jnp.dot(x, wu_ref[...].astype(BF16), preferred_element_type=F32)
            hmid = g * (1.0 / (1.0 + jnp.exp(-g))) * u
            part = jnp.dot(hmid.astype(BF16), wd_ref[...].astype(BF16), preferred_element_type=F32)

            @pl.when(j == 0)
            def _():
                acc_ref[rows, :] = part

            @pl.when(j > 0)
            def _():
                acc_ref[rows, :] += part

        @pl.when(active & (j == last_j))
        def _():
            wait_sends(s % 2)
            _slab_store(gouts[s % 2], acc_ref[rows, :], pad=False)
            start_send(s % 2, row0 + s * sub, n_rows)

    @pl.when((j == jnp.minimum(1, last_j)) & (v + 1 < nv_ref[0]))
    def _():
        start_fetch(vrow_ref[v + 1], jnp.minimum(vcnt_ref[v + 1], sub))

    @pl.when((v == pl.num_programs(0) - 1) & (j == last_j))
    def _():
        wait_sends(0)
        wait_sends(1)


def _moe(h_slab, eidx, w_gate, w_up, w_down, layer, T, D):
    E, F = w_gate.shape[1], w_gate.shape[-1]
    A = 2 * T
    pitch = _slab_pitch(D)
    fc = 256 if F % 256 == 0 else F
    visit_rows = FFN_VISIT_SUBS * FFN_SUB
    n_visits = E + A // visit_rows

    e_flat = eidx.reshape(A)
    order = jnp.argsort(e_flat, stable=True).astype(jnp.int32)
    experts = jnp.arange(E, dtype=jnp.int32)
    counts = jnp.sum((e_flat[:, None] == experts[None, :]).astype(jnp.int32), axis=0)
    start = jnp.cumsum(counts) - counts
    visits_e = (counts + visit_rows - 1) // visit_rows
    visit_end = jnp.cumsum(visits_e)
    n_used = visit_end[-1:].astype(jnp.int32)
    vid = jnp.minimum(jnp.arange(n_visits, dtype=jnp.int32), n_used[0] - 1)
    visit_expert = jnp.sum((vid[:, None] >= visit_end[None, :]).astype(jnp.int32), axis=1).astype(jnp.int32)
    k_in_expert = vid - (visit_end - visits_e)[visit_expert]
    visit_row0 = (start[visit_expert] + k_in_expert * visit_rows).astype(jnp.int32)
    visit_cnt = jnp.minimum(counts[visit_expert] - k_in_expert * visit_rows, visit_rows).astype(jnp.int32)

    n_fc = F // fc

    def chunk(v, j, nv):
        return jnp.where(v < nv[0], j, n_fc - 1)

    def w_map(v, j, ve, vrow, vcnt, nv, ordr, tok):
        return (layer, ve[v], 0, chunk(v, j, nv))

    out = pl.pallas_call(
        functools.partial(_ffn_body, D=D),
        grid_spec=pltpu.PrefetchScalarGridSpec(
            num_scalar_prefetch=6, grid=(n_visits, n_fc),
            in_specs=[pl.BlockSpec(memory_space=pl.ANY),
                      pl.BlockSpec(memory_space=pl.ANY),
                      pl.BlockSpec((None, None, D, fc), w_map),
                      pl.BlockSpec((None, None, D, fc), w_map),
                      pl.BlockSpec((None, None, fc, D),
                                   lambda v, j, ve, vrow, vcnt, nv, ordr, tok: (layer, ve[v], chunk(v, j, nv), 0))],
            out_specs=pl.BlockSpec(memory_space=pl.ANY),
            scratch_shapes=[pltpu.VMEM((FFN_SUB * pitch, LANES), F32), pltpu.VMEM((FFN_SUB * pitch, LANES), F32),
                            pltpu.VMEM((FFN_SUB * pitch, LANES), F32),
                            pltpu.VMEM((visit_rows, D), BF16), pltpu.VMEM((visit_rows, D), F32),
                            pltpu.SMEM((2,), jnp.int32), pltpu.SemaphoreType.DMA(()),
                            pltpu.SemaphoreType.DMA((2,))]),
        out_shape=jax.ShapeDtypeStruct((A, pitch, LANES), F32),
        compiler_params=pltpu.CompilerParams(dimension_semantics=("arbitrary", "arbitrary"),
                                             vmem_limit_bytes=FFN_VMEM_LIMIT),
        name="expert_ffn",
    )(visit_expert, visit_row0, visit_cnt, n_used, order, jnp.where(order >= T, order - T, order),
      h_slab.reshape(T, pitch, LANES), h_slab, w_gate, w_up, w_down)
    return out.reshape(A * pitch, LANES)


def _proj_heads_body(x_ref, w_ref, ca_ref, sb_ref, sc_ref, o_ref, acc_ref, *, d, hb, rope):
    acc = jnp.dot(x_ref[...], w_ref[...].astype(BF16), preferred_element_type=F32)
    tm = x_ref.shape[0]
    n = tm // d
    half = ROPE_DIM // 2
    if d > 1:
        for hh in range(hb):
            acc_ref[hh * tm:(hh + 1) * tm, :] = acc[:, hh * HEAD_DIM:(hh + 1) * HEAD_DIM]
    for r in range(d):
        if rope:
            rows = pl.ds(r, n, stride=d) if d > 1 else pl.ds(0, n)
            ca, sb, sc = ca_ref[rows, :], sb_ref[rows, :], sc_ref[rows, :]
        for hh in range(hb):
            if d > 1:
                val = acc_ref[pl.ds(hh * tm + r, n, stride=d), :]
            else:
                val = acc[:, hh * HEAD_DIM:(hh + 1) * HEAD_DIM]
            if rope:
                val = (val * ca + pltpu.roll(val, half, 1) * sb
                       + pltpu.roll(val, HEAD_DIM - half, 1) * sc)
            o_ref[hh, r] = val.astype(o_ref.dtype)


def _proj_heads(x, w, layer, col0, n_heads, d, B, S, tables, rope, tm=1024, hb=4):
    M, K = x.shape
    tm = min(tm, S)
    tn = hb * HEAD_DIM
    assert S % tm == 0 and n_heads % hb == 0 and col0 % tn == 0 and (tm // d) % 16 == 0
    mt = S // tm
    c0 = col0 // tn
    tab_spec = pl.BlockSpec((tm, HEAD_DIM), lambda i, j: (i % mt, 0))
    return pl.pallas_call(
        functools.partial(_proj_heads_body, d=d, hb=hb, rope=rope),
        grid=(M // tm, n_heads // hb),
        in_specs=[pl.BlockSpec((tm, K), lambda i, j: (i, 0)),
                  pl.BlockSpec((None, K, tn), lambda i, j: (layer, 0, c0 + j)),
                  tab_spec, tab_spec, tab_spec],
        out_specs=pl.BlockSpec((None, hb, d, tm // d, HEAD_DIM), lambda i, j: (i // mt, j, 0, i % mt, 0)),
        out_shape=jax.ShapeDtypeStruct((B, n_heads, d, S // d, HEAD_DIM), BF16),
        scratch_shapes=[pltpu.VMEM((hb * tm, HEAD_DIM), F32)],
        compiler_params=_cparams("parallel", "parallel"),
        name="attn_proj_d%d" % d,
    )(x, w, *tables)


def _rope_tables(S, scale):
    half = ROPE_DIM // 2
    inv_freq = ROPE_THETA ** (-jnp.arange(half, dtype=F32) / half)
    ang = jnp.arange(S).astype(F32)[:, None] * inv_freq[None, :]
    cos, sin = jnp.cos(ang), jnp.sin(ang)
    zeros = jnp.zeros((S, HEAD_DIM - ROPE_DIM), F32)
    ca = jnp.concatenate([cos, cos, jnp.ones_like(zeros)], axis=1)
    sb = jnp.concatenate([jnp.zeros_like(sin), sin, zeros], axis=1)
    sc = jnp.concatenate([-sin, jnp.zeros_like(sin), zeros], axis=1)
    return ca * scale, sb * scale, sc * scale


def _attn_body(q_ref, kc_ref, kp_ref, vc_ref, vp_ref, o_ref, lse_ref, kbuf, vbuf):
    SB, LB, _ = q_ref.shape
    blk = ATT_BLOCK
    n = pl.program_id(1)
    kbuf[:, :blk] = kp_ref[...]
    kbuf[:, blk:] = kc_ref[...]
    vbuf[:, :blk] = vp_ref[...]
    vbuf[:, blk:] = vc_ref[...]

    row = lax.broadcasted_iota(jnp.int32, (blk, 2 * blk), 0)
    col = lax.broadcasted_iota(jnp.int32, (blk, 2 * blk), 1)
    band = (col >= row) & (col <= row + blk)
    band_first = band & ((col >= blk) | (n > 0))
    eye = (lax.broadcasted_iota(jnp.int32, (blk, blk), 0) == lax.broadcasted_iota(jnp.int32, (blk, blk), 1))

    for sb in range(SB):
        for qi in range(LB // blk):
            q = q_ref[sb, qi * blk:(qi + 1) * blk, :]
            kk = kbuf[sb, qi * blk:(qi + 2) * blk, :]
            vv = vbuf[sb, qi * blk:(qi + 2) * blk, :]
            s = lax.dot_general(q, kk, (((1,), (1,)), ((), ())), preferred_element_type=F32)
            s = jnp.where(band_first if qi == 0 else band, s, NEG)
            m = jnp.max(s, axis=1, keepdims=True)
            p = jnp.exp(s - m)
            l = jnp.sum(p, axis=1, keepdims=True)
            o = jnp.dot(p.astype(BF16), vv, preferred_element_type=F32) / l
            o_ref[sb, qi * blk:(qi + 1) * blk, :] = o.astype(o_ref.dtype)
            lse = m + jnp.log(l)
            lse_ref[sb, qi:qi + 1, :] = jnp.sum(
                jnp.where(eye, jnp.broadcast_to(lse, (blk, blk)), 0.0), axis=0, keepdims=True)


def _band_attention(q, k, v, rows_per_step=2048):
    NS, L, Dh = q.shape
    blk = ATT_BLOCK
    LB = min(L, rows_per_step)
    SB = max(1, rows_per_step // LB)
    assert L % LB == 0 and NS % SB == 0 and LB % blk == 0
    nlb = LB // blk
    cur = pl.BlockSpec((SB, LB, Dh), lambda s, n: (s, n, 0))
    prev = pl.BlockSpec((SB, blk, Dh), lambda s, n: (s, jnp.maximum(n * nlb - 1, 0), 0))
    return pl.pallas_call(
        _attn_body,
        grid=(NS // SB, L // LB),
        in_specs=[cur, cur, prev, cur, prev],
        out_specs=[cur, pl.BlockSpec((SB, nlb, blk), lambda s, n: (s, n, 0))],
        out_shape=[jax.ShapeDtypeStruct((NS, L, Dh), BF16),
                   jax.ShapeDtypeStruct((NS, L // blk, blk), F32)],
        scratch_shapes=[pltpu.VMEM((SB, LB + blk, Dh), BF16), pltpu.VMEM((SB, LB + blk, Dh), BF16)],
        compiler_params=_cparams("parallel", "arbitrary"),
        name="band_attention",
    )(q, k, k, v, v)


def _combine_body(*refs, dils):
    G = len(dils)
    o_refs, l_refs = refs[:G], refs[G:2 * G]
    out_ref, acc_ref, wcol_ref = refs[2 * G:]
    S = out_ref.shape[0]
    blk = LANES
    eye = (lax.broadcasted_iota(jnp.int32, (blk, blk), 0) == lax.broadcasted_iota(jnp.int32, (blk, blk), 1))

    lses = [l[...] for l in l_refs]
    mx = functools.reduce(jnp.maximum, lses)
    es = [jnp.exp(l - mx) for l in lses]
    tot = functools.reduce(lambda a, b: a + b, es)
    for g in range(G):
        mix = es[g] / tot
        for tb in range(S // blk):
            seg = jnp.broadcast_to(mix[:, tb * blk:(tb + 1) * blk], (blk, blk))
            col = jnp.sum(jnp.where(eye, seg, 0.0), axis=1, keepdims=True)
            wcol_ref[g * S + tb * blk:g * S + (tb + 1) * blk, :] = jnp.broadcast_to(col, (blk, blk))
    for g, d in enumerate(dils):
        n = S // d
        for r in range(d):
            rows = pl.ds(r, n, stride=d) if d > 1 else pl.ds(0, n)
            wrows = pl.ds(g * S + r, n, stride=d) if d > 1 else pl.ds(g * S, n)
            contrib = o_refs[g][r].astype(F32) * wcol_ref[wrows, :]
            if g == 0:
                acc_ref[rows, :] = contrib
            else:
                acc_ref[rows, :] += contrib
    out_ref[...] = acc_ref[...].astype(out_ref.dtype)


def _combine(os_, lses, B, S, n_heads):
    dils = tuple(d for _, d in ATT_GROUPS)
    in_specs = [pl.BlockSpec((None, None, d, S // d, HEAD_DIM), lambda b, h: (b, h, 0, 0, 0)) for d in dils]
    in_specs += [pl.BlockSpec((None, None, 1, S), lambda b, h: (b, h, 0, 0))] * len(dils)
    return pl.pallas_call(
        functools.partial(_combine_body, dils=dils),
        grid=(B, n_heads),
        in_specs=in_specs,
        out_specs=pl.BlockSpec((None, S, HEAD_DIM), lambda b, h: (b, 0, h)),
        out_shape=jax.ShapeDtypeStruct((B, S, n_heads * HEAD_DIM), BF16),
        scratch_shapes=[pltpu.VMEM((S, HEAD_DIM), F32), pltpu.VMEM((len(dils) * S, LANES), F32)],
        compiler_params=_cparams("parallel", "parallel"),
        name="group_combine",
    )(*os_, *lses)


def kernel(x, mlstm_w_in, mlstm_b_gate, mlstm_norm_g, mlstm_w_out, att_w_q, att_w_kv_shared, att_w_o,
           router_w, router_b, expert_w_gate, expert_w_up, expert_w_down, ln_g, ln_b):
    B, S, D = x.shape
    T = B * S
    depth = ln_g.shape[0]
    n_a = mlstm_w_in.shape[0]
    alpha = (2.0 * depth) ** 0.25

    Hm = mlstm_b_gate.shape[-1] // 2
    dv = mlstm_w_out.shape[1] // Hm
    dk = (mlstm_w_in.shape[-1] - 2 * Hm - 2 * Hm * dv) // (2 * Hm)
    n_main = 2 * Hm * dk + 2 * Hm * dv

    AH = att_w_o.shape[1] // HEAD_DIM
    G = len(ATT_GROUPS)
    AW = G * AH * HEAD_DIM
    E = router_w.shape[1]
    assert E == N_EXPERT_GROUPS * EXPERTS_PER_GROUP

    perm = (jnp.arange(E) % N_EXPERT_GROUPS) * EXPERTS_PER_GROUP + jnp.arange(E) // N_EXPERT_GROUPS
    router = (router_w.T[perm].astype(F32), router_b[perm].reshape(E, 1).astype(F32))

    h = x.reshape(T, D)
    h_b = h.astype(BF16)
    kv = None
    for layer in range(depth):
        if layer < n_a:
            w_in_t = jnp.swapaxes(mlstm_w_in, 1, 2)
            proj = _mm(h_b, w_in_t, layer, 0, n_main, BF16, w_transposed=True)
            gates_t = _mlstm_gates(h_b, w_in_t[layer, n_main:], mlstm_b_gate[layer])
            hm = _mlstm(proj.reshape(B, S, n_main), gates_t, mlstm_norm_g[layer], B, S, Hm, dk, dv)
            mix = _mm(hm.reshape(T, Hm * dv), mlstm_w_out, layer, 0, D, BF16)
        else:
            bl = layer - n_a
            k_tabs = _rope_tables(S, 1.0)
            q_tabs = _rope_tables(S, HEAD_DIM ** -0.5)
            w_kv = att_w_kv_shared.reshape(1, D, 2 * AW)
            os_, lses = [], []
            for g, (window, d) in enumerate(ATT_GROUPS):
                assert window // d == ATT_BLOCK and (S // d) % ATT_BLOCK == 0
                if kv is None or len(kv) <= g:
                    kv = (kv or []) + [(
                        _proj_heads(h_b, w_kv, 0, g * AH * HEAD_DIM, AH, d, B, S, k_tabs, True),
                        _proj_heads(h_b, w_kv, 0, AW + g * AH * HEAD_DIM, AH, d, B, S, k_tabs, False))]
                kg, vg = kv[g]
                qg = _proj_heads(h_b, att_w_q, bl, g * AH * HEAD_DIM, AH, d, B, S, q_tabs, True)
                L = S // d
                ns = B * AH * d
                o, lse = _band_attention(qg.reshape(ns, L, HEAD_DIM), kg.reshape(ns, L, HEAD_DIM),
                                         vg.reshape(ns, L, HEAD_DIM))
                os_.append(o.reshape(B, AH, d, L, HEAD_DIM))
                lses.append(lse.reshape(B, AH, d, L).transpose(0, 1, 3, 2).reshape(B, AH, 1, S))
            ob = _combine(os_, lses, B, S, AH)
            mix = _mm(ob.reshape(T, AH * HEAD_DIM), att_w_o, bl, 0, D, BF16)

        tm_ln = min(256, T)
        h1_slab, eidx, ew = _ln(T, D, [(h, False, 0), (mix, False, 0)], ln_g[layer, 0], ln_b[layer, 0],
                                alpha, ("slab",), router=router, tm=tm_ln)
        y2 = _moe(h1_slab, eidx, expert_w_gate, expert_w_up, expert_w_down, layer, T, D)
        last = layer == depth - 1
        outs = _ln(T, D, [(h1_slab, True, 0), (y2, True, 0), (y2, True, T // tm_ln)],
                   ln_g[layer, 1], ln_b[layer, 1], alpha, ("f32",) if last else ("f32", "bf16"),
                   add_w=ew, tm=tm_ln)
        h = outs[0]
        h_b = None if last else outs[1]
    return h.reshape(B, S, D)
```

```python
import functools
import math

import jax
import jax.numpy as jnp
from jax import lax
from jax.experimental import pallas as pl
from jax.experimental.pallas import tpu as pltpu

F32 = jnp.float32
BF16 = jnp.bfloat16

HEAD_DIM = 128
ATT_GROUPS = ((128, 1), (512, 4), (2048, 16))
ATT_BLOCK = 128
ROPE_DIM = HEAD_DIM // 4
ROPE_THETA = 500000.0
N_EXPERT_GROUPS = 8
EXPERTS_PER_GROUP = 4
NORM_EPS = 1e-5
MLSTM_CHUNK = 256
LANES = 128
SUBLANES = 8
VMEM_LIMIT = 56 * 1024 * 1024
NEG = -1e30


def _cparams(*sem):
    return pltpu.CompilerParams(dimension_semantics=sem, vmem_limit_bytes=VMEM_LIMIT)


def _mm_body(x_ref, w_ref, o_ref, *, w_transposed):
    contract = (((1,), (1 if w_transposed else 0,)), ((), ()))
    o_ref[...] = lax.dot_general(x_ref[...], w_ref[...].astype(BF16), contract,
                                 preferred_element_type=F32).astype(o_ref.dtype)


def _mm(x, w, layer, col0, ncols, out_dtype, w_transposed=False, tm=1024, tn=512):
    M, K = x.shape
    tm = min(tm, M)
    assert M % tm == 0 and ncols % tn == 0 and col0 % tn == 0
    c0 = col0 // tn
    if w_transposed:
        w_spec = pl.BlockSpec((None, tn, K), lambda i, j: (layer, c0 + j, 0))
    else:
        w_spec = pl.BlockSpec((None, K, tn), lambda i, j: (layer, 0, c0 + j))
    return pl.pallas_call(
        functools.partial(_mm_body, w_transposed=w_transposed),
        grid=(M // tm, ncols // tn),
        in_specs=[pl.BlockSpec((tm, K), lambda i, j: (i, 0)), w_spec],
        out_specs=pl.BlockSpec((tm, tn), lambda i, j: (i, j)),
        out_shape=jax.ShapeDtypeStruct((M, ncols), out_dtype),
        compiler_params=_cparams("parallel", "parallel"),
        name="dense_proj",
    )(x, w)


def _gates_body(wt_ref, x_ref, b_ref, o_ref, *, n_heads):
    g = lax.dot_general(wt_ref[...].astype(BF16), x_ref[...], (((1,), (1,)), ((), ())),
                        preferred_element_type=F32)
    g = g + b_ref[...]
    log_sig = jnp.minimum(g, 0.0) - jnp.log1p(jnp.exp(-jnp.abs(g)))
    row = lax.broadcasted_iota(jnp.int32, g.shape, 0)
    o_ref[...] = jnp.where(row >= n_heads, log_sig, g)


def _mlstm_gates(x, w_gates_t, b_gate, tm=512):
    T, D = x.shape
    H2 = w_gates_t.shape[0]
    tm = min(tm, T)
    return pl.pallas_call(
        functools.partial(_gates_body, n_heads=H2 // 2),
        grid=(T // tm,),
        in_specs=[pl.BlockSpec((H2, D), lambda i: (0, 0)),
                  pl.BlockSpec((tm, D), lambda i: (i, 0)),
                  pl.BlockSpec((H2, 1), lambda i: (0, 0))],
        out_specs=pl.BlockSpec((H2, tm), lambda i: (0, i)),
        out_shape=jax.ShapeDtypeStruct((H2, T), F32),
        compiler_params=_cparams("parallel"),
        name="mlstm_gates",
    )(w_gates_t, x, b_gate.reshape(H2, 1).astype(F32))


def _mlstm_body(q_ref, k_ref, v_ref, op_ref, li_ref, lf_ref, g_ref, out_ref,
                c_ref, n_ref, m_ref, *, scale):
    L = q_ref.shape[0]

    @pl.when(pl.program_id(2) == 0)
    def _():
        c_ref[...] = jnp.zeros_like(c_ref)
        n_ref[...] = jnp.zeros_like(n_ref)
        m_ref[...] = jnp.zeros_like(m_ref)

    q = q_ref[...]
    k = k_ref[...]
    v = v_ref[...]
    li = li_ref[...]
    lf = lf_ref[...]
    m_prev = m_ref[...]

    row = lax.broadcasted_iota(jnp.int32, (L, L), 0)
    col = lax.broadcasted_iota(jnp.int32, (L, L), 1)
    causal = col <= row
    eye = col == row

    def to_col(r):
        return jnp.sum(jnp.where(eye, jnp.broadcast_to(r, (L, L)), 0.0), axis=1, keepdims=True)

    b_col = jnp.sum(jnp.where(causal, jnp.broadcast_to(lf, (L, L)), 0.0), axis=1, keepdims=True)
    b_row = jnp.sum(jnp.where(eye, jnp.broadcast_to(b_col, (L, L)), 0.0), axis=0, keepdims=True)
    a_row = li - b_row
    a_b = jnp.broadcast_to(a_row, (L, L))
    mloc = jnp.maximum(m_prev, jnp.max(jnp.where(causal, a_b, NEG), axis=1, keepdims=True))
    dmat = jnp.exp(jnp.where(causal, a_b - mloc, NEG))
    inter = jnp.exp(m_prev - mloc)

    s = lax.dot_general(q, k, (((1,), (1,)), ((), ())), preferred_element_type=F32) * scale * dmat
    qc = jnp.dot(q, c_ref[...].astype(BF16), preferred_element_type=F32) * scale
    num = inter * qc + jnp.dot(s.astype(BF16), v, preferred_element_type=F32)
    qn = jnp.sum(q.astype(F32) * n_ref[...], axis=1, keepdims=True) * scale
    den = inter * qn + jnp.sum(s, axis=1, keepdims=True)
    m_t = b_col + mloc
    h = num / jnp.maximum(jnp.abs(den), jnp.exp(-m_t))

    m_last = jnp.maximum(m_prev, jnp.max(a_row, axis=1, keepdims=True))
    b_last = jnp.sum(lf, axis=1, keepdims=True)
    decay = jnp.exp(m_prev - m_last)
    w_col = to_col(jnp.exp(a_row - m_last))
    wk = w_col * k.astype(F32)
    c_ref[...] = decay * c_ref[...] + lax.dot_general(
        wk.astype(BF16), v, (((0,), (0,)), ((), ())), preferred_element_type=F32)
    n_ref[...] = decay * n_ref[...] + jnp.sum(wk, axis=0, keepdims=True)
    m_ref[...] = b_last + m_last

    hn = h * lax.rsqrt(jnp.mean(h * h, axis=1, keepdims=True) + NORM_EPS)
    gate = 1.0 / (1.0 + jnp.exp(-op_ref[...].astype(F32)))
    out_ref[...] = (hn * g_ref[...] * gate).astype(out_ref.dtype)


def _mlstm(proj, gates_t, norm_g, B, S, H, dk, dv):
    L = min(MLSTM_CHUNK, S)
    nc = S // L
    assert S % L == 0 and (2 * H * dk) % dv == 0
    v0 = (2 * H * dk) // dv
    gates3 = gates_t.reshape(2 * H, 1, B * S)
    return pl.pallas_call(
        functools.partial(_mlstm_body, scale=dk ** -0.5),
        grid=(B, H, nc),
        in_specs=[pl.BlockSpec((None, L, dk), lambda b, h, c: (b, c, h)),
                  pl.BlockSpec((None, L, dk), lambda b, h, c: (b, c, H + h)),
                  pl.BlockSpec((None, L, dv), lambda b, h, c: (b, c, v0 + h)),
                  pl.BlockSpec((None, L, dv), lambda b, h, c: (b, c, v0 + H + h)),
                  pl.BlockSpec((None, 1, L), lambda b, h, c: (h, 0, b * nc + c)),
                  pl.BlockSpec((None, 1, L), lambda b, h, c: (H + h, 0, b * nc + c)),
                  pl.BlockSpec((1, dv), lambda b, h, c: (0, h))],
        out_specs=pl.BlockSpec((None, L, dv), lambda b, h, c: (b, c, h)),
        out_shape=jax.ShapeDtypeStruct((B, S, H * dv), BF16),
        scratch_shapes=[pltpu.VMEM((dk, dv), F32), pltpu.VMEM((1, dk), F32), pltpu.VMEM((1, 1), F32)],
        compiler_params=_cparams("parallel", "parallel", "arbitrary"),
        name="mlstm_chunks",
    )(proj, proj, proj, proj, gates3, gates3, norm_g.reshape(1, H * dv).astype(F32))


def _slab_pitch(D):
    return D // LANES + SUBLANES


def _slab_load(ref, n_tok, D):
    pitch = _slab_pitch(D)
    return jnp.concatenate([ref[pl.ds(c, n_tok, stride=pitch), :] for c in range(D // LANES)], axis=1)


def _slab_store(ref, val, pad=True):
    n_tok, D = val.shape
    pitch = _slab_pitch(D)
    nrow = D // LANES
    for c in range(nrow):
        ref[pl.ds(c, n_tok, stride=pitch), :] = val[:, c * LANES:(c + 1) * LANES].astype(ref.dtype)
    if pad:
        for c in range(nrow, pitch):
            ref[pl.ds(c, n_tok, stride=pitch), :] = jnp.zeros((n_tok, LANES), ref.dtype)


def _ln_body(*refs, tm, slabs, weighted, alpha, router, outs):
    n_in = len(slabs)
    pos = n_in
    if weighted:
        aw = refs[pos][...]
        pos += 1
    g_ref, b_ref = refs[pos], refs[pos + 1]
    pos += 2
    D = g_ref.shape[1]
    vals = [_slab_load(r, tm, D) if sl else r[...].astype(F32) for r, sl in zip(refs[:n_in], slabs)]
    if router:
        wr_ref, br_ref = refs[pos], refs[pos + 1]
        pos += 2

    z = alpha * vals[0]
    for k, a in enumerate(vals[1:]):
        z = z + (a * aw[:, k:k + 1] if weighted else a)
    mu = jnp.mean(z, axis=1, keepdims=True)
    zc = z - mu
    var = jnp.mean(zc * zc, axis=1, keepdims=True)
    y = zc * lax.rsqrt(var + NORM_EPS) * g_ref[...] + b_ref[...]
    for kind in outs:
        if kind == "slab":
            _slab_store(refs[pos], y)
        else:
            refs[pos][...] = y.astype(refs[pos].dtype)
        pos += 1

    if router:
        ei_ref, ew_ref = refs[pos], refs[pos + 1]
        ng, epg = N_EXPERT_GROUPS, EXPERTS_PER_GROUP
        lt = lax.dot_general(wr_ref[...], y, (((1,), (1,)), ((), ())),
                             precision=lax.Precision.HIGHEST, preferred_element_type=F32) + br_ref[...]
        ex = jnp.exp(lt - jnp.max(lt, axis=0, keepdims=True))
        p = ex / jnp.sum(ex, axis=0, keepdims=True)
        p0, p1, p2, p3 = [p[j * ng:(j + 1) * ng] for j in range(epg)]
        a, b = jnp.maximum(p0, p1), jnp.minimum(p0, p1)
        c, d = jnp.maximum(p2, p3), jnp.minimum(p2, p3)
        score = jnp.maximum(a, c) + jnp.maximum(jnp.minimum(a, c), jnp.maximum(b, d))
        gi = lax.broadcasted_iota(jnp.int32, score.shape, 0)
        best = jnp.min(jnp.where(score == jnp.max(score, axis=0, keepdims=True), gi, ng),
                       axis=0, keepdims=True)
        sel = gi == best
        v = [jnp.sum(jnp.where(sel, pj, 0.0), axis=0, keepdims=True) for pj in (p0, p1, p2, p3)]

        def first_max(vals):
            top = jnp.maximum(jnp.maximum(vals[0], vals[1]), jnp.maximum(vals[2], vals[3]))
            idx = jnp.where(vals[0] == top, 0, jnp.where(vals[1] == top, 1, jnp.where(vals[2] == top, 2, 3)))
            return top, idx

        v1, j1 = first_max(v)
        v2, j2 = first_max([jnp.where(j1 == j, -1.0, v[j]) for j in range(epg)])
        tot = v1 + v2
        ei_ref[...] = jnp.concatenate([best * epg + j1, best * epg + j2], axis=0).astype(jnp.int32)
        eye = (lax.broadcasted_iota(jnp.int32, (tm, tm), 0) == lax.broadcasted_iota(jnp.int32, (tm, tm), 1))
        cols = [jnp.sum(jnp.where(eye, jnp.broadcast_to(w, (tm, tm)), 0.0), axis=1, keepdims=True)
                for w in (v1 / tot, v2 / tot)]
        lane = lax.broadcasted_iota(jnp.int32, (tm, LANES), 1)
        ew_ref[...] = jnp.where(lane == 0, cols[0], jnp.where(lane == 1, cols[1], 0.0))


def _ln(T, D, inputs, g, b, alpha, outs, add_w=None, router=None, tm=256):
    tm = min(tm, T)
    pitch = _slab_pitch(D)

    def spec(is_slab, off):
        if is_slab:
            return pl.BlockSpec((tm * pitch, LANES), lambda i: (off + i, 0))
        return pl.BlockSpec((tm, D), lambda i: (off + i, 0))

    in_specs = [spec(sl, off) for _, sl, off in inputs]
    args = [a for a, _, _ in inputs]
    if add_w is not None:
        in_specs.append(pl.BlockSpec((tm, LANES), lambda i: (i, 0)))
        args.append(add_w)
    in_specs += [pl.BlockSpec((1, D), lambda i: (0, 0))] * 2
    args += [g.reshape(1, D).astype(F32), b.reshape(1, D).astype(F32)]
    out_shape, out_specs = [], []
    for kind in outs:
        if kind == "slab":
            out_shape.append(jax.ShapeDtypeStruct((T * pitch, LANES), F32))
        else:
            out_shape.append(jax.ShapeDtypeStruct((T, D), F32 if kind == "f32" else BF16))
        out_specs.append(spec(kind == "slab", 0))
    if router is not None:
        wr_t, br = router
        E = wr_t.shape[0]
        in_specs += [pl.BlockSpec((E, D), lambda i: (0, 0)), pl.BlockSpec((E, 1), lambda i: (0, 0))]
        args += [wr_t, br]
        out_shape += [jax.ShapeDtypeStruct((2, T), jnp.int32), jax.ShapeDtypeStruct((T, LANES), F32)]
        out_specs += [pl.BlockSpec((2, tm), lambda i: (0, i)), pl.BlockSpec((tm, LANES), lambda i: (i, 0))]
    return pl.pallas_call(
        functools.partial(_ln_body, tm=tm, slabs=tuple(sl for _, sl, _ in inputs),
                          weighted=add_w is not None, alpha=alpha,
                          router=router is not None, outs=tuple(outs)),
        grid=(T // tm,),
        in_specs=in_specs, out_specs=out_specs, out_shape=out_shape,
        compiler_params=_cparams("parallel"),
        name="deepnorm_ln_router" if router is not None else "deepnorm_ln",
    )(*args)


FFN_SUB = 256
FFN_VISIT_SUBS = 3
FFN_VMEM_LIMIT = 62 * 1024 * 1024


def _ffn_body(ve_ref, vrow_ref, vcnt_ref, nv_ref, ord_ref, tok_ref, h_hbm, hflat_hbm, wg_ref, wu_ref, wd_ref,
              y_hbm, gin_ref, gout0_ref, gout1_ref, xb_ref, acc_ref, pend_ref, sem_in, sem_out, *, D):
    v, j = pl.program_id(0), pl.program_id(1)
    n_fc = pl.num_programs(1)
    last_j = n_fc - 1
    pitch = _slab_pitch(D)
    n_data = D // LANES
    sub = FFN_SUB
    used = v < nv_ref[0]
    cnt, row0 = vcnt_ref[v], vrow_ref[v]

    def slab_row(ref, r):
        return ref.at[pl.ds(pl.multiple_of(r * pitch, SUBLANES), n_data)]

    def wait_rows(n, row_copy, rows_per_copy, vmem_ref, sem):
        @pl.when(n == sub)
        def _():
            whole = pl.ds(0, sub * rows_per_copy)
            pltpu.make_async_copy(hflat_hbm.at[whole], vmem_ref.at[whole], sem).wait()

        @pl.when(n < sub)
        def _():
            def body(r, carry):
                row_copy(r).wait()
                return carry
            lax.fori_loop(0, n, body, 0)

    def start_rows(n, row_copy):
        def pair(i, carry):
            row_copy(2 * i).start(priority=0)
            row_copy(2 * i + 1).start(priority=1)
            return carry
        lax.fori_loop(0, lax.shift_right_logical(n, 1), pair, 0)

        @pl.when((n & 1) == 1)
        def _():
            row_copy(n - 1).start(priority=0)

    def start_fetch(first_row, n):
        start_rows(n, lambda r: pltpu.make_async_copy(
            h_hbm.at[tok_ref[first_row + r], pl.ds(0, n_data)], slab_row(gin_ref, r), sem_in))

    def wait_fetch(n):
        wait_rows(n, lambda r: pltpu.make_async_copy(h_hbm.at[0, pl.ds(0, n_data)], slab_row(gin_ref, r), sem_in),
                  n_data, gin_ref, sem_in)

    gouts = (gout0_ref, gout1_ref)

    def out_row(b, r):
        return gouts[b].at[pl.ds(pl.multiple_of(r * pitch, SUBLANES), pitch)]

    def start_send(b, first_row, n):
        start_rows(n, lambda r: pltpu.make_async_copy(
            out_row(b, r), y_hbm.at[ord_ref[first_row + r]], sem_out.at[b]))
        pend_ref[b] = n

    def wait_sends(b):
        wait_rows(pend_ref[b], lambda r: pltpu.make_async_copy(out_row(b, r), y_hbm.at[0], sem_out.at[b]),
                  pitch, gouts[b], sem_out.at[b])
        pend_ref[b] = 0

    @pl.when((v == 0) & (j == 0))
    def _():
        gin_ref[...] = jnp.zeros_like(gin_ref)
        xb_ref[...] = jnp.zeros_like(xb_ref)
        for b in range(2):
            gouts[b][...] = jnp.zeros_like(gouts[b])
            pend_ref[b] = 0
        start_fetch(row0, jnp.minimum(cnt, sub))

    for s in range(FFN_VISIT_SUBS):
        n_rows = jnp.minimum(cnt - s * sub, sub)
        active = used & (s * sub < cnt)
        rows = slice(s * sub, (s + 1) * sub)

        @pl.when(active & (j == 0))
        def _():
            wait_fetch(n_rows)
            xb_ref[rows, :] = _slab_load(gin_ref, sub, D).astype(BF16)
            if s + 1 < FFN_VISIT_SUBS:
                @pl.when((s + 1) * sub < cnt)
                def _():
                    start_fetch(row0 + (s + 1) * sub, jnp.minimum(cnt - (s + 1) * sub, sub))

        @pl.when(active)
        def _():
            x = xb_ref[rows, :]
            g = jnp.dot(x, wg_ref[...].astype(BF16), preferred_element_type=F32)
            u = jnp.dot(x, wu_ref[...].astype(BF16), preferred_element_type=F32)
            hmid = g * (1.0 / (1.0 + jnp.exp(-g))) * u
            part = jnp.dot(hmid.astype(BF16), wd_ref[...].astype(BF16), preferred_element_type=F32)

            @pl.when(j == 0)
            def _():
                acc_ref[rows, :] = part

            @pl.when(j > 0)
            def _():
                acc_ref[rows, :] += part

        @pl.when(active & (j == last_j))
        def _():
            wait_sends(s % 2)
            _slab_store(gouts[s % 2], acc_ref[rows, :], pad=False)
            start_send(s % 2, row0 + s * sub, n_rows)

    @pl.when((j == jnp.minimum(1, last_j)) & (v + 1 < nv_ref[0]))
    def _():
        start_fetch(vrow_ref[v + 1], jnp.minimum(vcnt_ref[v + 1], sub))

    @pl.when((v == pl.num_programs(0) - 1) & (j == last_j))
    def _():
        wait_sends(0)
        wait_sends(1)


def _moe(h_slab, eidx, w_gate, w_up, w_down, layer, T, D):
    E, F = w_gate.shape[1], w_gate.shape[-1]
    A = 2 * T
    pitch = _slab_pitch(D)
    fc = 256 if F % 256 == 0 else F
    visit_rows = FFN_VISIT_SUBS * FFN_SUB
    n_visits = E + A // visit_rows

    e_flat = eidx.reshape(A)
    order = jnp.argsort(e_flat, stable=True).astype(jnp.int32)
    experts = jnp.arange(E, dtype=jnp.int32)
    counts = jnp.sum((e_flat[:, None] == experts[None, :]).astype(jnp.int32), axis=0)
    start = jnp.cumsum(counts) - counts
    visits_e = (counts + visit_rows - 1) // visit_rows
    visit_end = jnp.cumsum(visits_e)
    n_used = visit_end[-1:].astype(jnp.int32)
    vid = jnp.minimum(jnp.arange(n_visits, dtype=jnp.int32), n_used[0] - 1)
    visit_expert = jnp.sum((vid[:, None] >= visit_end[None, :]).astype(jnp.int32), axis=1).astype(jnp.int32)
    k_in_expert = vid - (visit_end - visits_e)[visit_expert]
    visit_row0 = (start[visit_expert] + k_in_expert * visit_rows).astype(jnp.int32)
    visit_cnt = jnp.minimum(counts[visit_expert] - k_in_expert * visit_rows, visit_rows).astype(jnp.int32)

    n_fc = F // fc

    def chunk(v, j, nv):
        return jnp.where(v < nv[0], j, n_fc - 1)

    def w_map(v, j, ve, vrow, vcnt, nv, ordr, tok):
        return (layer, ve[v], 0, chunk(v, j, nv))

    out = pl.pallas_call(
        functools.partial(_ffn_body, D=D),
        grid_spec=pltpu.PrefetchScalarGridSpec(
            num_scalar_prefetch=6, grid=(n_visits, n_fc),
            in_specs=[pl.BlockSpec(memory_space=pl.ANY),
                      pl.BlockSpec(memory_space=pl.ANY),
                      pl.BlockSpec((None, None, D, fc), w_map),
                      pl.BlockSpec((None, None, D, fc), w_map),
                      pl.BlockSpec((None, None, fc, D),
                                   lambda v, j, ve, vrow, vcnt, nv, ordr, tok: (layer, ve[v], chunk(v, j, nv), 0))],
            out_specs=pl.BlockSpec(memory_space=pl.ANY),
            scratch_shapes=[pltpu.VMEM((FFN_SUB * pitch, LANES), F32), pltpu.VMEM((FFN_SUB * pitch, LANES), F32),
                            pltpu.VMEM((FFN_SUB * pitch, LANES), F32),
                            pltpu.VMEM((visit_rows, D), BF16), pltpu.VMEM((visit_rows, D), F32),
                            pltpu.SMEM((2,), jnp.int32), pltpu.SemaphoreType.DMA(()),
                            pltpu.SemaphoreType.DMA((2,))]),
        out_shape=jax.ShapeDtypeStruct((A, pitch, LANES), F32),
        compiler_params=pltpu.CompilerParams(dimension_semantics=("arbitrary", "arbitrary"),
                                             vmem_limit_bytes=FFN_VMEM_LIMIT),
        name="expert_ffn",
    )(visit_expert, visit_row0, visit_cnt, n_used, order, jnp.where(order >= T, order - T, order),
      h_slab.reshape(T, pitch, LANES), h_slab, w_gate, w_up, w_down)
    return out.reshape(A * pitch, LANES)


def _proj_heads_body(x_ref, w_ref, ca_ref, sb_ref, sc_ref, o_ref, acc_ref, *, d, hb, rope):
    acc = jnp.dot(x_ref[...], w_ref[...].astype(BF16), preferred_element_type=F32)
    tm = x_ref.shape[0]
    n = tm // d
    half = ROPE_DIM // 2
    if d > 1:
        for hh in range(hb):
            acc_ref[hh * tm:(hh + 1) * tm, :] = acc[:, hh * HEAD_DIM:(hh + 1) * HEAD_DIM]
    for r in range(d):
        if rope:
            rows = pl.ds(r, n, stride=d) if d > 1 else pl.ds(0, n)
            ca, sb, sc = ca_ref[rows, :], sb_ref[rows, :], sc_ref[rows, :]
        for hh in range(hb):
            if d > 1:
                val = acc_ref[pl.ds(hh * tm + r, n, stride=d), :]
            else:
                val = acc[:, hh * HEAD_DIM:(hh + 1) * HEAD_DIM]
            if rope:
                val = (val * ca + pltpu.roll(val, half, 1) * sb
                       + pltpu.roll(val, HEAD_DIM - half, 1) * sc)
            o_ref[hh, r] = val.astype(o_ref.dtype)


def _proj_heads(x, w, layer, col0, n_heads, d, B, S, tables, rope, tm=1024, hb=4):
    M, K = x.shape
    tm = min(tm, S)
    tn = hb * HEAD_DIM
    assert S % tm == 0 and n_heads % hb == 0 and col0 % tn == 0 and (tm // d) % 16 == 0
    mt = S // tm
    c0 = col0 // tn
    tab_spec = pl.BlockSpec((tm, HEAD_DIM), lambda i, j: (i % mt, 0))
    return pl.pallas_call(
        functools.partial(_proj_heads_body, d=d, hb=hb, rope=rope),
        grid=(M // tm, n_heads // hb),
        in_specs=[pl.BlockSpec((tm, K), lambda i, j: (i, 0)),
                  pl.BlockSpec((None, K, tn), lambda i, j: (layer, 0, c0 + j)),
                  tab_spec, tab_spec, tab_spec],
        out_specs=pl.BlockSpec((None, hb, d, tm // d, HEAD_DIM), lambda i, j: (i // mt, j, 0, i % mt, 0)),
        out_shape=jax.ShapeDtypeStruct((B, n_heads, d, S // d, HEAD_DIM), BF16),
        scratch_shapes=[pltpu.VMEM((hb * tm, HEAD_DIM), F32)],
        compiler_params=_cparams("parallel", "parallel"),
        name="attn_proj_d%d" % d,
    )(x, w, *tables)


def _rope_tables(S, scale):
    half = ROPE_DIM // 2
    inv_freq = ROPE_THETA ** (-jnp.arange(half, dtype=F32) / half)
    ang = jnp.arange(S).astype(F32)[:, None] * inv_freq[None, :]
    cos, sin = jnp.cos(ang), jnp.sin(ang)
    zeros = jnp.zeros((S, HEAD_DIM - ROPE_DIM), F32)
    ca = jnp.concatenate([cos, cos, jnp.ones_like(zeros)], axis=1)
    sb = jnp.concatenate([jnp.zeros_like(sin), sin, zeros], axis=1)
    sc = jnp.concatenate([-sin, jnp.zeros_like(sin), zeros], axis=1)
    return ca * scale, sb * scale, sc * scale


def _attn_body(q_ref, kc_ref, kp_ref, vc_ref, vp_ref, o_ref, lse_ref, kbuf, vbuf):
    SB, LB, _ = q_ref.shape
    blk = ATT_BLOCK
    n = pl.program_id(1)
    kbuf[:, :blk] = kp_ref[...]
    kbuf[:, blk:] = kc_ref[...]
    vbuf[:, :blk] = vp_ref[...]
    vbuf[:, blk:] = vc_ref[...]

    row = lax.broadcasted_iota(jnp.int32, (blk, 2 * blk), 0)
    col = lax.broadcasted_iota(jnp.int32, (blk, 2 * blk), 1)
    band = (col >= row) & (col <= row + blk)
    band_first = band & ((col >= blk) | (n > 0))
    eye = (lax.broadcasted_iota(jnp.int32, (blk, blk), 0) == lax.broadcasted_iota(jnp.int32, (blk, blk), 1))

    for sb in range(SB):
        for qi in range(LB // blk):
            q = q_ref[sb, qi * blk:(qi + 1) * blk, :]
            kk = kbuf[sb, qi * blk:(qi + 2) * blk, :]
            vv = vbuf[sb, qi * blk:(qi + 2) * blk, :]
            s = lax.dot_general(q, kk, (((1,), (1,)), ((), ())), preferred_element_type=F32)
            s = jnp.where(band_first if qi == 0 else band, s, NEG)
            m = jnp.max(s, axis=1, keepdims=True)
            p = jnp.exp(s - m)
            l = jnp.sum(p, axis=1, keepdims=True)
            o = jnp.dot(p.astype(BF16), vv, preferred_element_type=F32) / l
            o_ref[sb, qi * blk:(qi + 1) * blk, :] = o.astype(o_ref.dtype)
            lse = m + jnp.log(l)
            lse_ref[sb, qi:qi + 1, :] = jnp.sum(
                jnp.where(eye, jnp.broadcast_to(lse, (blk, blk)), 0.0), axis=0, keepdims=True)


def _band_attention(q, k, v, rows_per_step=2048):
    NS, L, Dh = q.shape
    blk = ATT_BLOCK
    LB = min(L, rows_per_step)
    SB = max(1, rows_per_step // LB)
    assert L % LB == 0 and NS % SB == 0 and LB % blk == 0
    nlb = LB // blk
    cur = pl.BlockSpec((SB, LB, Dh), lambda s, n: (s, n, 0))
    prev = pl.BlockSpec((SB, blk, Dh), lambda s, n: (s, jnp.maximum(n * nlb - 1, 0), 0))
    return pl.pallas_call(
        _attn_body,
        grid=(NS // SB, L // LB),
        in_specs=[cur, cur, prev, cur, prev],
        out_specs=[cur, pl.BlockSpec((SB, nlb, blk), lambda s, n: (s, n, 0))],
        out_shape=[jax.ShapeDtypeStruct((NS, L, Dh), BF16),
                   jax.ShapeDtypeStruct((NS, L // blk, blk), F32)],
        scratch_shapes=[pltpu.VMEM((SB, LB + blk, Dh), BF16), pltpu.VMEM((SB, LB + blk, Dh), BF16)],
        compiler_params=_cparams("parallel", "arbitrary"),
        name="band_attention",
    )(q, k, k, v, v)


def _combine_body(*refs, dils):
    G = len(dils)
    o_refs, l_refs = refs[:G], refs[G:2 * G]
    out_ref, acc_ref, wcol_ref = refs[2 * G:]
    S = out_ref.shape[0]
    blk = LANES
    eye = (lax.broadcasted_iota(jnp.int32, (blk, blk), 0) == lax.broadcasted_iota(jnp.int32, (blk, blk), 1))

    lses = [l[...] for l in l_refs]
    mx = functools.reduce(jnp.maximum, lses)
    es = [jnp.exp(l - mx) for l in lses]
    tot = functools.reduce(lambda a, b: a + b, es)
    for g in range(G):
        mix = es[g] / tot
        for tb in range(S // blk):
            seg = jnp.broadcast_to(mix[:, tb * blk:(tb + 1) * blk], (blk, blk))
            col = jnp.sum(jnp.where(eye, seg, 0.0), axis=1, keepdims=True)
            wcol_ref[g * S + tb * blk:g * S + (tb + 1) * blk, :] = jnp.broadcast_to(col, (blk, blk))
    for g, d in enumerate(dils):
        n = S // d
        for r in range(d):
            rows = pl.ds(r, n, stride=d) if d > 1 else pl.ds(0, n)
            wrows = pl.ds(g * S + r, n, stride=d) if d > 1 else pl.ds(g * S, n)
            contrib = o_refs[g][r].astype(F32) * wcol_ref[wrows, :]
            if g == 0:
                acc_ref[rows, :] = contrib
            else:
                acc_ref[rows, :] += contrib
    out_ref[...] = acc_ref[...].astype(out_ref.dtype)


def _combine(os_, lses, B, S, n_heads):
    dils = tuple(d for _, d in ATT_GROUPS)
    in_specs = [pl.BlockSpec((None, None, d, S // d, HEAD_DIM), lambda b, h: (b, h, 0, 0, 0)) for d in dils]
    in_specs += [pl.BlockSpec((None, None, 1, S), lambda b, h: (b, h, 0, 0))] * len(dils)
    return pl.pallas_call(
        functools.partial(_combine_body, dils=dils),
        grid=(B, n_heads),
        in_specs=in_specs,
        out_specs=pl.BlockSpec((None, S, HEAD_DIM), lambda b, h: (b, 0, h)),
        out_shape=jax.ShapeDtypeStruct((B, S, n_heads * HEAD_DIM), BF16),
        scratch_shapes=[pltpu.VMEM((S, HEAD_DIM), F32), pltpu.VMEM((len(dils) * S, LANES), F32)],
        compiler_params=_cparams("parallel", "parallel"),
        name="group_combine",
    )(*os_, *lses)


def kernel(x, mlstm_w_in, mlstm_b_gate, mlstm_norm_g, mlstm_w_out, att_w_q, att_w_kv_shared, att_w_o,
           router_w, router_b, expert_w_gate, expert_w_up, expert_w_down, ln_g, ln_b):
    B, S, D = x.shape
    T = B * S
    depth = ln_g.shape[0]
    n_a = mlstm_w_in.shape[0]
    alpha = (2.0 * depth) ** 0.25

    Hm = mlstm_b_gate.shape[-1] // 2
    dv = mlstm_w_out.shape[1] // Hm
    dk = (mlstm_w_in.shape[-1] - 2 * Hm - 2 * Hm * dv) // (2 * Hm)
    n_main = 2 * Hm * dk + 2 * Hm * dv

    AH = att_w_o.shape[1] // HEAD_DIM
    G = len(ATT_GROUPS)
    AW = G * AH * HEAD_DIM
    E = router_w.shape[1]
    assert E == N_EXPERT_GROUPS * EXPERTS_PER_GROUP

    perm = (jnp.arange(E) % N_EXPERT_GROUPS) * EXPERTS_PER_GROUP + jnp.arange(E) // N_EXPERT_GROUPS
    router = (router_w.T[perm].astype(F32), router_b[perm].reshape(E, 1).astype(F32))

    h = x.reshape(T, D)
    h_b = h.astype(BF16)
    kv = None
    for layer in range(depth):
        if layer < n_a:
            w_in_t = jnp.swapaxes(mlstm_w_in, 1, 2)
            proj = _mm(h_b, w_in_t, layer, 0, n_main, BF16, w_transposed=True)
            gates_t = _mlstm_gates(h_b, w_in_t[layer, n_main:], mlstm_b_gate[layer])
            hm = _mlstm(proj.reshape(B, S, n_main), gates_t, mlstm_norm_g[layer], B, S, Hm, dk, dv)
            mix = _mm(hm.reshape(T, Hm * dv), mlstm_w_out, layer, 0, D, BF16)
        else:
            bl = layer - n_a
            k_tabs = _rope_tables(S, 1.0)
            q_tabs = _rope_tables(S, HEAD_DIM ** -0.5)
            w_kv = att_w_kv_shared.reshape(1, D, 2 * AW)
            os_, lses = [], []
            for g, (window, d) in enumerate(ATT_GROUPS):
                assert window // d == ATT_BLOCK and (S // d) % ATT_BLOCK == 0
                if kv is None or len(kv) <= g:
                    kv = (kv or []) + [(
                        _proj_heads(h_b, w_kv, 0, g * AH * HEAD_DIM, AH, d, B, S, k_tabs, True),
                        _proj_heads(h_b, w_kv, 0, AW + g * AH * HEAD_DIM, AH, d, B, S, k_tabs, False))]
                kg, vg = kv[g]
                qg = _proj_heads(h_b, att_w_q, bl, g * AH * HEAD_DIM, AH, d, B, S, q_tabs, True)
                L = S // d
                ns = B * AH * d
                o, lse = _band_attention(qg.reshape(ns, L, HEAD_DIM), kg.reshape(ns, L, HEAD_DIM),
                                         vg.reshape(ns, L, HEAD_DIM))
                os_.append(o.reshape(B, AH, d, L, HEAD_DIM))
                lses.append(lse.reshape(B, AH, d, L).transpose(0, 1, 3, 2).reshape(B, AH, 1, S))
            ob = _combine(os_, lses, B, S, AH)
            mix = _mm(ob.reshape(T, AH * HEAD_DIM), att_w_o, bl, 0, D, BF16)

        tm_ln = min(256, T)
        h1_slab, eidx, ew = _ln(T, D, [(h, False, 0), (mix, False, 0)], ln_g[layer, 0], ln_b[layer, 0],
                                alpha, ("slab",), router=router, tm=tm_ln)
        y2 = _moe(h1_slab, eidx, expert_w_gate, expert_w_up, expert_w_down, layer, T, D)
        last = layer == depth - 1
        outs = _ln(T, D, [(h1_slab, True, 0), (y2, True, 0), (y2, True, T // tm_ln)],
                   ln_g[layer, 1], ln_b[layer, 1], alpha, ("f32",) if last else ("f32", "bf16"),
                   add_w=ew, tm=tm_ln)
        h = outs[0]
        h_b = None if last else outs[1]
    return h.reshape(B, S, D)
```

```python
import functools
import math

import jax
import jax.numpy as jnp
from jax import lax
from jax.experimental import pallas as pl
from jax.experimental.pallas import tpu as pltpu

F32 = jnp.float32
BF16 = jnp.bfloat16

HEAD_DIM = 128
ATT_GROUPS = ((128, 1), (512, 4), (2048, 16))
ATT_BLOCK = 128
ROPE_DIM = HEAD_DIM // 4
ROPE_THETA = 500000.0
N_EXPERT_GROUPS = 8
EXPERTS_PER_GROUP = 4
NORM_EPS = 1e-5
MLSTM_CHUNK = 256
LANES = 128
SUBLANES = 8
VMEM_LIMIT = 56 * 1024 * 1024
NEG = -1e30


def _cparams(*sem):
    return pltpu.CompilerParams(dimension_semantics=sem, vmem_limit_bytes=VMEM_LIMIT)


def _mm_body(x_ref, w_ref, o_ref, *, w_transposed):
    contract = (((1,), (1 if w_transposed else 0,)), ((), ()))
    o_ref[...] = lax.dot_general(x_ref[...], w_ref[...].astype(BF16), contract,
                                 preferred_element_type=F32).astype(o_ref.dtype)


def _mm(x, w, layer, col0, ncols, out_dtype, w_transposed=False, tm=1024, tn=512):
    M, K = x.shape
    tm = min(tm, M)
    assert M % tm == 0 and ncols % tn == 0 and col0 % tn == 0
    c0 = col0 // tn
    if w_transposed:
        w_spec = pl.BlockSpec((None, tn, K), lambda i, j: (layer, c0 + j, 0))
    else:
        w_spec = pl.BlockSpec((None, K, tn), lambda i, j: (layer, 0, c0 + j))
    return pl.pallas_call(
        functools.partial(_mm_body, w_transposed=w_transposed),
        grid=(M // tm, ncols // tn),
        in_specs=[pl.BlockSpec((tm, K), lambda i, j: (i, 0)), w_spec],
        out_specs=pl.BlockSpec((tm, tn), lambda i, j: (i, j)),
        out_shape=jax.ShapeDtypeStruct((M, ncols), out_dtype),
        compiler_params=_cparams("parallel", "parallel"),
        name="dense_proj",
    )(x, w)


def _gates_body(wt_ref, x_ref, b_ref, o_ref, *, n_heads):
    g = lax.dot_general(wt_ref[...].astype(BF16), x_ref[...], (((1,), (1,)), ((), ())),
                        preferred_element_type=F32)
    g = g + b_ref[...]
    log_sig = jnp.minimum(g, 0.0) - jnp.log1p(jnp.exp(-jnp.abs(g)))
    row = lax.broadcasted_iota(jnp.int32, g.shape, 0)
    o_ref[...] = jnp.where(row >= n_heads, log_sig, g)


def _mlstm_gates(x, w_gates_t, b_gate, tm=512):
    T, D = x.shape
    H2 = w_gates_t.shape[0]
    tm = min(tm, T)
    return pl.pallas_call(
        functools.partial(_gates_body, n_heads=H2 // 2),
        grid=(T // tm,),
        in_specs=[pl.BlockSpec((H2, D), lambda i: (0, 0)),
                  pl.BlockSpec((tm, D), lambda i: (i, 0)),
                  pl.BlockSpec((H2, 1), lambda i: (0, 0))],
        out_specs=pl.BlockSpec((H2, tm), lambda i: (0, i)),
        out_shape=jax.ShapeDtypeStruct((H2, T), F32),
        compiler_params=_cparams("parallel"),
        name="mlstm_gates",
    )(w_gates_t, x, b_gate.reshape(H2, 1).astype(F32))


def _mlstm_body(q_ref, k_ref, v_ref, op_ref, li_ref, lf_ref, g_ref, out_ref,
                c_ref, n_ref, m_ref, *, scale):
    L = q_ref.shape[0]

    @pl.when(pl.program_id(2) == 0)
    def _():
        c_ref[...] = jnp.zeros_like(c_ref)
        n_ref[...] = jnp.zeros_like(n_ref)
        m_ref[...] = jnp.zeros_like(m_ref)

    q = q_ref[...]
    k = k_ref[...]
    v = v_ref[...]
    li = li_ref[...]
    lf = lf_ref[...]
    m_prev = m_ref[...]

    row = lax.broadcasted_iota(jnp.int32, (L, L), 0)
    col = lax.broadcasted_iota(jnp.int32, (L, L), 1)
    causal = col <= row
    eye = col == row

    def to_col(r):
        return jnp.sum(jnp.where(eye, jnp.broadcast_to(r, (L, L)), 0.0), axis=1, keepdims=True)

    b_col = jnp.sum(jnp.where(causal, jnp.broadcast_to(lf, (L, L)), 0.0), axis=1, keepdims=True)
    b_row = jnp.sum(jnp.where(eye, jnp.broadcast_to(b_col, (L, L)), 0.0), axis=0, keepdims=True)
    a_row = li - b_row
    a_b = jnp.broadcast_to(a_row, (L, L))
    mloc = jnp.maximum(m_prev, jnp.max(jnp.where(causal, a_b, NEG), axis=1, keepdims=True))
    dmat = jnp.exp(jnp.where(causal, a_b - mloc, NEG))
    inter = jnp.exp(m_prev - mloc)

    s = lax.dot_general(q, k, (((1,), (1,)), ((), ())), preferred_element_type=F32) * scale * dmat
    qc = jnp.dot(q, c_ref[...].astype(BF16), preferred_element_type=F32) * scale
    num = inter * qc + jnp.dot(s.astype(BF16), v, preferred_element_type=F32)
    qn = jnp.sum(q.astype(F32) * n_ref[...], axis=1, keepdims=True) * scale
    den = inter * qn + jnp.sum(s, axis=1, keepdims=True)
    m_t = b_col + mloc
    h = num / jnp.maximum(jnp.abs(den), jnp.exp(-m_t))

    m_last = jnp.maximum(m_prev, jnp.max(a_row, axis=1, keepdims=True))
    b_last = jnp.sum(lf, axis=1, keepdims=True)
    decay = jnp.exp(m_prev - m_last)
    w_col = to_col(jnp.exp(a_row - m_last))
    wk = w_col * k.astype(F32)
    c_ref[...] = decay * c_ref[...] + lax.dot_general(
        wk.astype(BF16), v, (((0,), (0,)), ((), ())), preferred_element_type=F32)
    n_ref[...] = decay * n_ref[...] + jnp.sum(wk, axis=0, keepdims=True)
    m_ref[...] = b_last + m_last

    hn = h * lax.rsqrt(jnp.mean(h * h, axis=1, keepdims=True) + NORM_EPS)
    gate = 1.0 / (1.0 + jnp.exp(-op_ref[...].astype(F32)))
    out_ref[...] = (hn * g_ref[...] * gate).astype(out_ref.dtype)


def _mlstm(proj, gates_t, norm_g, B, S, H, dk, dv):
    L = min(MLSTM_CHUNK, S)
    nc = S // L
    assert S % L == 0 and (2 * H * dk) % dv == 0
    v0 = (2 * H * dk) // dv
    gates3 = gates_t.reshape(2 * H, 1, B * S)
    return pl.pallas_call(
        functools.partial(_mlstm_body, scale=dk ** -0.5),
        grid=(B, H, nc),
        in_specs=[pl.BlockSpec((None, L, dk), lambda b, h, c: (b, c, h)),
                  pl.BlockSpec((None, L, dk), lambda b, h, c: (b, c, H + h)),
                  pl.BlockSpec((None, L, dv), lambda b, h, c: (b, c, v0 + h)),
                  pl.BlockSpec((None, L, dv), lambda b, h, c: (b, c, v0 + H + h)),
                  pl.BlockSpec((None, 1, L), lambda b, h, c: (h, 0, b * nc + c)),
                  pl.BlockSpec((None, 1, L), lambda b, h, c: (H + h, 0, b * nc + c)),
                  pl.BlockSpec((1, dv), lambda b, h, c: (0, h))],
        out_specs=pl.BlockSpec((None, L, dv), lambda b, h, c: (b, c, h)),
        out_shape=jax.ShapeDtypeStruct((B, S, H * dv), BF16),
        scratch_shapes=[pltpu.VMEM((dk, dv), F32), pltpu.VMEM((1, dk), F32), pltpu.VMEM((1, 1), F32)],
        compiler_params=_cparams("parallel", "parallel", "arbitrary"),
        name="mlstm_chunks",
    )(proj, proj, proj, proj, gates3, gates3, norm_g.reshape(1, H * dv).astype(F32))


def _slab_pitch(D):
    return D // LANES + SUBLANES


def _slab_load(ref, n_tok, D):
    pitch = _slab_pitch(D)
    return jnp.concatenate([ref[pl.ds(c, n_tok, stride=pitch), :] for c in range(D // LANES)], axis=1)


def _slab_store(ref, val, pad=True):
    n_tok, D = val.shape
    pitch = _slab_pitch(D)
    nrow = D // LANES
    for c in range(nrow):
        ref[pl.ds(c, n_tok, stride=pitch), :] = val[:, c * LANES:(c + 1) * LANES].astype(ref.dtype)
    if pad:
        for c in range(nrow, pitch):
            ref[pl.ds(c, n_tok, stride=pitch), :] = jnp.zeros((n_tok, LANES), ref.dtype)


def _ln_body(*refs, tm, slabs, weighted, alpha, router, outs):
    n_in = len(slabs)
    pos = n_in
    if weighted:
        aw = refs[pos][...]
        pos += 1
    g_ref, b_ref = refs[pos], refs[pos + 1]
    pos += 2
    D = g_ref.shape[1]
    vals = [_slab_load(r, tm, D) if sl else r[...].astype(F32) for r, sl in zip(refs[:n_in], slabs)]
    if router:
        wr_ref, br_ref = refs[pos], refs[pos + 1]
        pos += 2

    z = alpha * vals[0]
    for k, a in enumerate(vals[1:]):
        z = z + (a * aw[:, k:k + 1] if weighted else a)
    mu = jnp.mean(z, axis=1, keepdims=True)
    zc = z - mu
    var = jnp.mean(zc * zc, axis=1, keepdims=True)
    y = zc * lax.rsqrt(var + NORM_EPS) * g_ref[...] + b_ref[...]
    for kind in outs:
        if kind == "slab":
            _slab_store(refs[pos], y)
        else:
            refs[pos][...] = y.astype(refs[pos].dtype)
        pos += 1

    if router:
        ei_ref, ew_ref = refs[pos], refs[pos + 1]
        ng, epg = N_EXPERT_GROUPS, EXPERTS_PER_GROUP
        def split(a):
            hi = a.astype(BF16)
            return hi, (a - hi.astype(F32)).astype(BF16)

        def nt_dot(a, b):
            return lax.dot_general(a, b, (((1,), (1,)), ((), ())), preferred_element_type=F32)

        (w_hi, w_lo), (y_hi, y_lo) = split(wr_ref[...]), split(y)
        lt = nt_dot(w_hi, y_hi) + (nt_dot(w_hi, y_lo) + nt_dot(w_lo, y_hi)) + br_ref[...]
        ex = jnp.exp(lt - jnp.max(lt, axis=0, keepdims=True))
        p = ex / jnp.sum(ex, axis=0, keepdims=True)
        p0, p1, p2, p3 = [p[j * ng:(j + 1) * ng] for j in range(epg)]
        a, b = jnp.maximum(p0, p1), jnp.minimum(p0, p1)
        c, d = jnp.maximum(p2, p3), jnp.minimum(p2, p3)
        score = jnp.maximum(a, c) + jnp.maximum(jnp.minimum(a, c), jnp.maximum(b, d))
        gi = lax.broadcasted_iota(jnp.int32, score.shape, 0)
        best = jnp.min(jnp.where(score == jnp.max(score, axis=0, keepdims=True), gi, ng),
                       axis=0, keepdims=True)
        sel = gi == best
        v = [jnp.sum(jnp.where(sel, pj, 0.0), axis=0, keepdims=True) for pj in (p0, p1, p2, p3)]

        def first_max(vals):
            top = jnp.maximum(jnp.maximum(vals[0], vals[1]), jnp.maximum(vals[2], vals[3]))
            idx = jnp.where(vals[0] == top, 0, jnp.where(vals[1] == top, 1, jnp.where(vals[2] == top, 2, 3)))
            return top, idx

        v1, j1 = first_max(v)
        v2, j2 = first_max([jnp.where(j1 == j, -1.0, v[j]) for j in range(epg)])
        tot = v1 + v2
        ei_ref[...] = jnp.concatenate([best * epg + j1, best * epg + j2], axis=0).astype(jnp.int32)
        eye = (lax.broadcasted_iota(jnp.int32, (tm, tm), 0) == lax.broadcasted_iota(jnp.int32, (tm, tm), 1))
        cols = [jnp.sum(jnp.where(eye, jnp.broadcast_to(w, (tm, tm)), 0.0), axis=1, keepdims=True)
                for w in (v1 / tot, v2 / tot)]
        lane = lax.broadcasted_iota(jnp.int32, (tm, LANES), 1)
        ew_ref[...] = jnp.where(lane == 0, cols[0], jnp.where(lane == 1, cols[1], 0.0))


def _ln(T, D, inputs, g, b, alpha, outs, add_w=None, router=None, tm=256):
    tm = min(tm, T)
    pitch = _slab_pitch(D)

    def spec(is_slab, off):
        if is_slab:
            return pl.BlockSpec((tm * pitch, LANES), lambda i: (off + i, 0))
        return pl.BlockSpec((tm, D), lambda i: (off + i, 0))

    in_specs = [spec(sl, off) for _, sl, off in inputs]
    args = [a for a, _, _ in inputs]
    if add_w is not None:
        in_specs.append(pl.BlockSpec((tm, LANES), lambda i: (i, 0)))
        args.append(add_w)
    in_specs += [pl.BlockSpec((1, D), lambda i: (0, 0))] * 2
    args += [g.reshape(1, D).astype(F32), b.reshape(1, D).astype(F32)]
    out_shape, out_specs = [], []
    for kind in outs:
        if kind == "slab":
            out_shape.append(jax.ShapeDtypeStruct((T * pitch, LANES), F32))
        else:
            out_shape.append(jax.ShapeDtypeStruct((T, D), F32 if kind == "f32" else BF16))
        out_specs.append(spec(kind == "slab", 0))
    if router is not None:
        wr_t, br = router
        E = wr_t.shape[0]
        in_specs += [pl.BlockSpec((E, D), lambda i: (0, 0)), pl.BlockSpec((E, 1), lambda i: (0, 0))]
        args += [wr_t, br]
        out_shape += [jax.ShapeDtypeStruct((2, T), jnp.int32), jax.ShapeDtypeStruct((T, LANES), F32)]
        out_specs += [pl.BlockSpec((2, tm), lambda i: (0, i)), pl.BlockSpec((tm, LANES), lambda i: (i, 0))]
    return pl.pallas_call(
        functools.partial(_ln_body, tm=tm, slabs=tuple(sl for _, sl, _ in inputs),
                          weighted=add_w is not None, alpha=alpha,
                          router=router is not None, outs=tuple(outs)),
        grid=(T // tm,),
        in_specs=in_specs, out_specs=out_specs, out_shape=out_shape,
        compiler_params=_cparams("parallel"),
        name="deepnorm_ln_router" if router is not None else "deepnorm_ln",
    )(*args)


FFN_SUB = 256
FFN_VISIT_SUBS = 3
FFN_VMEM_LIMIT = 62 * 1024 * 1024


def _ffn_body(ve_ref, vrow_ref, vcnt_ref, nv_ref, ord_ref, tok_ref, h_hbm, hflat_hbm, wg_ref, wu_ref, wd_ref,
              y_hbm, gin_ref, gout0_ref, gout1_ref, xb_ref, acc_ref, pend_ref, sem_in, sem_out, *, D):
    v, j = pl.program_id(0), pl.program_id(1)
    n_fc = pl.num_programs(1)
    last_j = n_fc - 1
    pitch = _slab_pitch(D)
    n_data = D // LANES
    sub = FFN_SUB
    used = v < nv_ref[0]
    cnt, row0 = vcnt_ref[v], vrow_ref[v]

    def slab_row(ref, r):
        return ref.at[pl.ds(pl.multiple_of(r * pitch, SUBLANES), n_data)]

    def wait_rows(n, row_copy, rows_per_copy, vmem_ref, sem):
        @pl.when(n == sub)
        def _():
            whole = pl.ds(0, sub * rows_per_copy)
            pltpu.make_async_copy(hflat_hbm.at[whole], vmem_ref.at[whole], sem).wait()

        @pl.when(n < sub)
        def _():
            def body(r, carry):
                row_copy(r).wait()
                return carry
            lax.fori_loop(0, n, body, 0)

    def start_rows(n, row_copy):
        def pair(i, carry):
            row_copy(2 * i).start(priority=0)
            row_copy(2 * i + 1).start(priority=1)
            return carry
        lax.fori_loop(0, lax.shift_right_logical(n, 1), pair, 0)

        @pl.when((n & 1) == 1)
        def _():
            row_copy(n - 1).start(priority=0)

    def start_fetch(first_row, n):
        start_rows(n, lambda r: pltpu.make_async_copy(
            h_hbm.at[tok_ref[first_row + r], pl.ds(0, n_data)], slab_row(gin_ref, r), sem_in))

    def wait_fetch(n):
        wait_rows(n, lambda r: pltpu.make_async_copy(h_hbm.at[0, pl.ds(0, n_data)], slab_row(gin_ref, r), sem_in),
                  n_data, gin_ref, sem_in)

    gouts = (gout0_ref, gout1_ref)

    def out_row(b, r):
        return gouts[b].at[pl.ds(pl.multiple_of(r * pitch, SUBLANES), pitch)]

    def start_send(b, first_row, n):
        start_rows(n, lambda r: pltpu.make_async_copy(
            out_row(b, r), y_hbm.at[ord_ref[first_row + r]], sem_out.at[b]))
        pend_ref[b] = n

    def wait_sends(b):
        wait_rows(pend_ref[b], lambda r: pltpu.make_async_copy(out_row(b, r), y_hbm.at[0], sem_out.at[b]),
                  pitch, gouts[b], sem_out.at[b])
        pend_ref[b] = 0

    @pl.when((v == 0) & (j == 0))
    def _():
        gin_ref[...] = jnp.zeros_like(gin_ref)
        xb_ref[...] = jnp.zeros_like(xb_ref)
        for b in range(2):
            gouts[b][...] = jnp.zeros_like(gouts[b])
            pend_ref[b] = 0
        start_fetch(row0, jnp.minimum(cnt, sub))

    for s in range(FFN_VISIT_SUBS):
        n_rows = jnp.minimum(cnt - s * sub, sub)
        active = used & (s * sub < cnt)
        rows = slice(s * sub, (s + 1) * sub)

        @pl.when(active & (j == 0))
        def _():
            wait_fetch(n_rows)
            xb_ref[rows, :] = _slab_load(gin_ref, sub, D).astype(BF16)
            if s + 1 < FFN_VISIT_SUBS:
                @pl.when((s + 1) * sub < cnt)
                def _():
                    start_fetch(row0 + (s + 1) * sub, jnp.minimum(cnt - (s + 1) * sub, sub))

        @pl.when(active)
        def _():
            x = xb_ref[rows, :]
            g = jnp.dot(x, wg_ref[...].astype(BF16), preferred_element_type=F32)
            u = jnp.dot(x, wu_ref[...].astype(BF16), preferred_element_type=F32)
            hmid = g * (1.0 / (1.0 + jnp.exp(-g))) * u
            part = jnp.dot(hmid.astype(BF16), wd_ref[...].astype(BF16), preferred_element_type=F32)

            @pl.when(j == 0)
            def _():
                acc_ref[rows, :] = part

            @pl.when(j > 0)
            def _():
                acc_ref[rows, :] += part

        @pl.when(active & (j == last_j))
        def _():
            wait_sends(s % 2)
            _slab_store(gouts[s % 2], acc_ref[rows, :], pad=False)
            start_send(s % 2, row0 + s * sub, n_rows)

    @pl.when((j == jnp.minimum(1, last_j)) & (v + 1 < nv_ref[0]))
    def _():
        start_fetch(vrow_ref[v + 1], jnp.minimum(vcnt_ref[v + 1], sub))

    @pl.when((v == pl.num_programs(0) - 1) & (j == last_j))
    def _():
        wait_sends(0)
        wait_sends(1)


def _moe(h_slab, eidx, w_gate, w_up, w_down, layer, T, D):
    E, F = w_gate.shape[1], w_gate.shape[-1]
    A = 2 * T
    pitch = _slab_pitch(D)
    fc = 256 if F % 256 == 0 else F
    visit_rows = FFN_VISIT_SUBS * FFN_SUB
    n_visits = E + A // visit_rows

    e_flat = eidx.reshape(A)
    order = jnp.argsort(e_flat, stable=True).astype(jnp.int32)
    experts = jnp.arange(E, dtype=jnp.int32)
    counts = jnp.sum((e_flat[:, None] == experts[None, :]).astype(jnp.int32), axis=0)
    start = jnp.cumsum(counts) - counts
    visits_e = (counts + visit_rows - 1) // visit_rows
    visit_end = jnp.cumsum(visits_e)
    n_used = visit_end[-1:].astype(jnp.int32)
    vid = jnp.minimum(jnp.arange(n_visits, dtype=jnp.int32), n_used[0] - 1)
    visit_expert = jnp.sum((vid[:, None] >= visit_end[None, :]).astype(jnp.int32), axis=1).astype(jnp.int32)
    k_in_expert = vid - (visit_end - visits_e)[visit_expert]
    visit_row0 = (start[visit_expert] + k_in_expert * visit_rows).astype(jnp.int32)
    visit_cnt = jnp.minimum(counts[visit_expert] - k_in_expert * visit_rows, visit_rows).astype(jnp.int32)

    n_fc = F // fc

    def chunk(v, j, nv):
        return jnp.where(v < nv[0], j, n_fc - 1)

    def w_map(v, j, ve, vrow, vcnt, nv, ordr, tok):
        return (layer, ve[v], 0, chunk(v, j, nv))

    out = pl.pallas_call(
        functools.partial(_ffn_body, D=D),
        grid_spec=pltpu.PrefetchScalarGridSpec(
            num_scalar_prefetch=6, grid=(n_visits, n_fc),
            in_specs=[pl.BlockSpec(memory_space=pl.ANY),
                      pl.BlockSpec(memory_space=pl.ANY),
                      pl.BlockSpec((None, None, D, fc), w_map),
                      pl.BlockSpec((None, None, D, fc), w_map),
                      pl.BlockSpec((None, None, fc, D),
                                   lambda v, j, ve, vrow, vcnt, nv, ordr, tok: (layer, ve[v], chunk(v, j, nv), 0))],
            out_specs=pl.BlockSpec(memory_space=pl.ANY),
            scratch_shapes=[pltpu.VMEM((FFN_SUB * pitch, LANES), F32), pltpu.VMEM((FFN_SUB * pitch, LANES), F32),
                            pltpu.VMEM((FFN_SUB * pitch, LANES), F32),
                            pltpu.VMEM((visit_rows, D), BF16), pltpu.VMEM((visit_rows, D), F32),
                            pltpu.SMEM((2,), jnp.int32), pltpu.SemaphoreType.DMA(()),
                            pltpu.SemaphoreType.DMA((2,))]),
        out_shape=jax.ShapeDtypeStruct((A, pitch, LANES), F32),
        compiler_params=pltpu.CompilerParams(dimension_semantics=("arbitrary", "arbitrary"),
                                             vmem_limit_bytes=FFN_VMEM_LIMIT),
        name="expert_ffn",
    )(visit_expert, visit_row0, visit_cnt, n_used, order, jnp.where(order >= T, order - T, order),
      h_slab.reshape(T, pitch, LANES), h_slab, w_gate, w_up, w_down)
    return out.reshape(A * pitch, LANES)


def _proj_heads_body(x_ref, w_ref, ca_ref, sb_ref, sc_ref, o_ref, acc_ref, *, d, hb, rope):
    acc = jnp.dot(x_ref[...], w_ref[...].astype(BF16), preferred_element_type=F32)
    tm = x_ref.shape[0]
    n = tm // d
    half = ROPE_DIM // 2
    if d > 1:
        for hh in range(hb):
            acc_ref[hh * tm:(hh + 1) * tm, :] = acc[:, hh * HEAD_DIM:(hh + 1) * HEAD_DIM]
    for r in range(d):
        if rope:
            rows = pl.ds(r, n, stride=d) if d > 1 else pl.ds(0, n)
            ca, sb, sc = ca_ref[rows, :], sb_ref[rows, :], sc_ref[rows, :]
        for hh in range(hb):
            if d > 1:
                val = acc_ref[pl.ds(hh * tm + r, n, stride=d), :]
            else:
                val = acc[:, hh * HEAD_DIM:(hh + 1) * HEAD_DIM]
            if rope:
                val = (val * ca + pltpu.roll(val, half, 1) * sb
                       + pltpu.roll(val, HEAD_DIM - half, 1) * sc)
            o_ref[hh, r] = val.astype(o_ref.dtype)


def _proj_heads(x, w, layer, col0, n_heads, d, B, S, tables, rope, tm=1024, hb=4):
    M, K = x.shape
    tm = min(tm, S)
    tn = hb * HEAD_DIM
    assert S % tm == 0 and n_heads % hb == 0 and col0 % tn == 0 and (tm // d) % 16 == 0
    mt = S // tm
    c0 = col0 // tn
    tab_spec = pl.BlockSpec((tm, HEAD_DIM), lambda i, j: (i % mt, 0))
    return pl.pallas_call(
        functools.partial(_proj_heads_body, d=d, hb=hb, rope=rope),
        grid=(M // tm, n_heads // hb),
        in_specs=[pl.BlockSpec((tm, K), lambda i, j: (i, 0)),
                  pl.BlockSpec((None, K, tn), lambda i, j: (layer, 0, c0 + j)),
                  tab_spec, tab_spec, tab_spec],
        out_specs=pl.BlockSpec((None, hb, d, tm // d, HEAD_DIM), lambda i, j: (i // mt, j, 0, i % mt, 0)),
        out_shape=jax.ShapeDtypeStruct((B, n_heads, d, S // d, HEAD_DIM), BF16),
        scratch_shapes=[pltpu.VMEM((hb * tm, HEAD_DIM), F32)],
        compiler_params=_cparams("parallel", "parallel"),
        name="attn_proj_d%d" % d,
    )(x, w, *tables)


def _rope_tables(S, scale):
    half = ROPE_DIM // 2
    inv_freq = ROPE_THETA ** (-jnp.arange(half, dtype=F32) / half)
    ang = jnp.arange(S).astype(F32)[:, None] * inv_freq[None, :]
    cos, sin = jnp.cos(ang), jnp.sin(ang)
    zeros = jnp.zeros((S, HEAD_DIM - ROPE_DIM), F32)
    ca = jnp.concatenate([cos, cos, jnp.ones_like(zeros)], axis=1)
    sb = jnp.concatenate([jnp.zeros_like(sin), sin, zeros], axis=1)
    sc = jnp.concatenate([-sin, jnp.zeros_like(sin), zeros], axis=1)
    return ca * scale, sb * scale, sc * scale


def _attn_body(q_ref, kc_ref, kp_ref, vc_ref, vp_ref, o_ref, lse_ref, kbuf, vbuf):
    SB, LB, _ = q_ref.shape
    blk = ATT_BLOCK
    n = pl.program_id(1)
    kbuf[:, :blk] = kp_ref[...]
    kbuf[:, blk:] = kc_ref[...]
    vbuf[:, :blk] = vp_ref[...]
    vbuf[:, blk:] = vc_ref[...]

    row = lax.broadcasted_iota(jnp.int32, (blk, 2 * blk), 0)
    col = lax.broadcasted_iota(jnp.int32, (blk, 2 * blk), 1)
    band = (col >= row) & (col <= row + blk)
    band_first = band & ((col >= blk) | (n > 0))
    eye = (lax.broadcasted_iota(jnp.int32, (blk, blk), 0) == lax.broadcasted_iota(jnp.int32, (blk, blk), 1))

    for sb in range(SB):
        for qi in range(LB // blk):
            q = q_ref[sb, qi * blk:(qi + 1) * blk, :]
            kk = kbuf[sb, qi * blk:(qi + 2) * blk, :]
            vv = vbuf[sb, qi * blk:(qi + 2) * blk, :]
            s = lax.dot_general(q, kk, (((1,), (1,)), ((), ())), preferred_element_type=F32)
            s = jnp.where(band_first if qi == 0 else band, s, NEG)
            m = jnp.max(s, axis=1, keepdims=True)
            p = jnp.exp(s - m)
            l = jnp.sum(p, axis=1, keepdims=True)
            o = jnp.dot(p.astype(BF16), vv, preferred_element_type=F32) / l
            o_ref[sb, qi * blk:(qi + 1) * blk, :] = o.astype(o_ref.dtype)
            lse = m + jnp.log(l)
            lse_ref[sb, qi:qi + 1, :] = jnp.sum(
                jnp.where(eye, jnp.broadcast_to(lse, (blk, blk)), 0.0), axis=0, keepdims=True)


def _band_attention(q, k, v, rows_per_step=2048):
    NS, L, Dh = q.shape
    blk = ATT_BLOCK
    LB = min(L, rows_per_step)
    SB = max(1, rows_per_step // LB)
    assert L % LB == 0 and NS % SB == 0 and LB % blk == 0
    nlb = LB // blk
    cur = pl.BlockSpec((SB, LB, Dh), lambda s, n: (s, n, 0))
    prev = pl.BlockSpec((SB, blk, Dh), lambda s, n: (s, jnp.maximum(n * nlb - 1, 0), 0))
    return pl.pallas_call(
        _attn_body,
        grid=(NS // SB, L // LB),
        in_specs=[cur, cur, prev, cur, prev],
        out_specs=[cur, pl.BlockSpec((SB, nlb, blk), lambda s, n: (s, n, 0))],
        out_shape=[jax.ShapeDtypeStruct((NS, L, Dh), BF16),
                   jax.ShapeDtypeStruct((NS, L // blk, blk), F32)],
        scratch_shapes=[pltpu.VMEM((SB, LB + blk, Dh), BF16), pltpu.VMEM((SB, LB + blk, Dh), BF16)],
        compiler_params=_cparams("parallel", "arbitrary"),
        name="band_attention",
    )(q, k, k, v, v)


def _combine_body(*refs, dils):
    G = len(dils)
    o_refs, l_refs = refs[:G], refs[G:2 * G]
    out_ref, acc_ref, wcol_ref = refs[2 * G:]
    S = out_ref.shape[0]
    blk = LANES
    eye = (lax.broadcasted_iota(jnp.int32, (blk, blk), 0) == lax.broadcasted_iota(jnp.int32, (blk, blk), 1))

    lses = [l[...] for l in l_refs]
    mx = functools.reduce(jnp.maximum, lses)
    es = [jnp.exp(l - mx) for l in lses]
    tot = functools.reduce(lambda a, b: a + b, es)
    for g in range(G):
        mix = es[g] / tot
        for tb in range(S // blk):
            seg = jnp.broadcast_to(mix[:, tb * blk:(tb + 1) * blk], (blk, blk))
            col = jnp.sum(jnp.where(eye, seg, 0.0), axis=1, keepdims=True)
            wcol_ref[g * S + tb * blk:g * S + (tb + 1) * blk, :] = jnp.broadcast_to(col, (blk, blk))
    for g, d in enumerate(dils):
        n = S // d
        for r in range(d):
            rows = pl.ds(r, n, stride=d) if d > 1 else pl.ds(0, n)
            wrows = pl.ds(g * S + r, n, stride=d) if d > 1 else pl.ds(g * S, n)
            contrib = o_refs[g][r].astype(F32) * wcol_ref[wrows, :]
            if g == 0:
                acc_ref[rows, :] = contrib
            else:
                acc_ref[rows, :] += contrib
    out_ref[...] = acc_ref[...].astype(out_ref.dtype)


def _combine(os_, lses, B, S, n_heads):
    dils = tuple(d for _, d in ATT_GROUPS)
    in_specs = [pl.BlockSpec((None, None, d, S // d, HEAD_DIM), lambda b, h: (b, h, 0, 0, 0)) for d in dils]
    in_specs += [pl.BlockSpec((None, None, 1, S), lambda b, h: (b, h, 0, 0))] * len(dils)
    return pl.pallas_call(
        functools.partial(_combine_body, dils=dils),
        grid=(B, n_heads),
        in_specs=in_specs,
        out_specs=pl.BlockSpec((None, S, HEAD_DIM), lambda b, h: (b, 0, h)),
        out_shape=jax.ShapeDtypeStruct((B, S, n_heads * HEAD_DIM), BF16),
        scratch_shapes=[pltpu.VMEM((S, HEAD_DIM), F32), pltpu.VMEM((len(dils) * S, LANES), F32)],
        compiler_params=_cparams("parallel", "parallel"),
        name="group_combine",
    )(*os_, *lses)


def kernel(x, mlstm_w_in, mlstm_b_gate, mlstm_norm_g, mlstm_w_out, att_w_q, att_w_kv_shared, att_w_o,
           router_w, router_b, expert_w_gate, expert_w_up, expert_w_down, ln_g, ln_b):
    B, S, D = x.shape
    T = B * S
    depth = ln_g.shape[0]
    n_a = mlstm_w_in.shape[0]
    alpha = (2.0 * depth) ** 0.25

    Hm = mlstm_b_gate.shape[-1] // 2
    dv = mlstm_w_out.shape[1] // Hm
    dk = (mlstm_w_in.shape[-1] - 2 * Hm - 2 * Hm * dv) // (2 * Hm)
    n_main = 2 * Hm * dk + 2 * Hm * dv

    AH = att_w_o.shape[1] // HEAD_DIM
    G = len(ATT_GROUPS)
    AW = G * AH * HEAD_DIM
    E = router_w.shape[1]
    assert E == N_EXPERT_GROUPS * EXPERTS_PER_GROUP

    perm = (jnp.arange(E) % N_EXPERT_GROUPS) * EXPERTS_PER_GROUP + jnp.arange(E) // N_EXPERT_GROUPS
    router = (router_w.T[perm].astype(F32), router_b[perm].reshape(E, 1).astype(F32))

    h = x.reshape(T, D)
    h_b = h.astype(BF16)
    kv = None
    for layer in range(depth):
        if layer < n_a:
            w_in_t = jnp.swapaxes(mlstm_w_in, 1, 2)
            proj = _mm(h_b, w_in_t, layer, 0, n_main, BF16, w_transposed=True)
            gates_t = _mlstm_gates(h_b, w_in_t[layer, n_main:], mlstm_b_gate[layer])
            hm = _mlstm(proj.reshape(B, S, n_main), gates_t, mlstm_norm_g[layer], B, S, Hm, dk, dv)
            mix = _mm(hm.reshape(T, Hm * dv), mlstm_w_out, layer, 0, D, BF16)
        else:
            bl = layer - n_a
            k_tabs = _rope_tables(S, 1.0)
            q_tabs = _rope_tables(S, HEAD_DIM ** -0.5)
            w_kv = att_w_kv_shared.reshape(1, D, 2 * AW)
            os_, lses = [], []
            for g, (window, d) in enumerate(ATT_GROUPS):
                assert window // d == ATT_BLOCK and (S // d) % ATT_BLOCK == 0
                if kv is None or len(kv) <= g:
                    kv = (kv or []) + [(
                        _proj_heads(h_b, w_kv, 0, g * AH * HEAD_DIM, AH, d, B, S, k_tabs, True),
                        _proj_heads(h_b, w_kv, 0, AW + g * AH * HEAD_DIM, AH, d, B, S, k_tabs, False))]
                kg, vg = kv[g]
                qg = _proj_heads(h_b, att_w_q, bl, g * AH * HEAD_DIM, AH, d, B, S, q_tabs, True)
                L = S // d
                ns = B * AH * d
                o, lse = _band_attention(qg.reshape(ns, L, HEAD_DIM), kg.reshape(ns, L, HEAD_DIM),
                                         vg.reshape(ns, L, HEAD_DIM))
                os_.append(o.reshape(B, AH, d, L, HEAD_DIM))
                lses.append(lse.reshape(B, AH, d, L).transpose(0, 1, 3, 2).reshape(B, AH, 1, S))
            ob = _combine(os_, lses, B, S, AH)
            mix = _mm(ob.reshape(T, AH * HEAD_DIM), att_w_o, bl, 0, D, BF16)

        tm_ln = min(256, T)
        h1_slab, eidx, ew = _ln(T, D, [(h, False, 0), (mix, False, 0)], ln_g[layer, 0], ln_b[layer, 0],
                                alpha, ("slab",), router=router, tm=tm_ln)
        y2 = _moe(h1_slab, eidx, expert_w_gate, expert_w_up, expert_w_down, layer, T, D)
        last = layer == depth - 1
        outs = _ln(T, D, [(h1_slab, True, 0), (y2, True, 0), (y2, True, T // tm_ln)],
                   ln_g[layer, 1], ln_b[layer, 1], alpha, ("f32",) if last else ("f32", "bf16"),
                   add_w=ew, tm=tm_ln)
        h = outs[0]
        h_b = None if last else outs[1]
    return h.reshape(B, S, D)
```

```python
import functools
import math

import jax
import jax.numpy as jnp
from jax import lax
from jax.experimental import pallas as pl
from jax.experimental.pallas import tpu as pltpu

F32 = jnp.float32
BF16 = jnp.bfloat16

HEAD_DIM = 128
ATT_GROUPS = ((128, 1), (512, 4), (2048, 16))
ATT_BLOCK = 128
ROPE_DIM = HEAD_DIM // 4
ROPE_THETA = 500000.0
N_EXPERT_GROUPS = 8
EXPERTS_PER_GROUP = 4
NORM_EPS = 1e-5
MLSTM_CHUNK = 256
LANES = 128
SUBLANES = 8
VMEM_LIMIT = 56 * 1024 * 1024
NEG = -1e30


def _cparams(*sem):
    return pltpu.CompilerParams(dimension_semantics=sem, vmem_limit_bytes=VMEM_LIMIT)


def _mm_body(x_ref, w_ref, o_ref, *, w_transposed):
    contract = (((1,), (1 if w_transposed else 0,)), ((), ()))
    o_ref[...] = lax.dot_general(x_ref[...], w_ref[...].astype(BF16), contract,
                                 preferred_element_type=F32).astype(o_ref.dtype)


def _mm(x, w, layer, col0, ncols, out_dtype, w_transposed=False, tm=1024, tn=512):
    M, K = x.shape
    tm = min(tm, M)
    assert M % tm == 0 and ncols % tn == 0 and col0 % tn == 0
    c0 = col0 // tn
    if w_transposed:
        w_spec = pl.BlockSpec((None, tn, K), lambda i, j: (layer, c0 + j, 0))
    else:
        w_spec = pl.BlockSpec((None, K, tn), lambda i, j: (layer, 0, c0 + j))
    return pl.pallas_call(
        functools.partial(_mm_body, w_transposed=w_transposed),
        grid=(M // tm, ncols // tn),
        in_specs=[pl.BlockSpec((tm, K), lambda i, j: (i, 0)), w_spec],
        out_specs=pl.BlockSpec((tm, tn), lambda i, j: (i, j)),
        out_shape=jax.ShapeDtypeStruct((M, ncols), out_dtype),
        compiler_params=_cparams("parallel", "parallel"),
        name="dense_proj",
    )(x, w)


def _gates_body(wt_ref, x_ref, b_ref, o_ref, *, n_heads):
    g = lax.dot_general(wt_ref[...].astype(BF16), x_ref[...], (((1,), (1,)), ((), ())),
                        preferred_element_type=F32)
    g = g + b_ref[...]
    log_sig = jnp.minimum(g, 0.0) - jnp.log1p(jnp.exp(-jnp.abs(g)))
    row = lax.broadcasted_iota(jnp.int32, g.shape, 0)
    o_ref[...] = jnp.where(row >= n_heads, log_sig, g)


def _mlstm_gates(x, w_gates_t, b_gate, tm=512):
    T, D = x.shape
    H2 = w_gates_t.shape[0]
    tm = min(tm, T)
    return pl.pallas_call(
        functools.partial(_gates_body, n_heads=H2 // 2),
        grid=(T // tm,),
        in_specs=[pl.BlockSpec((H2, D), lambda i: (0, 0)),
                  pl.BlockSpec((tm, D), lambda i: (i, 0)),
                  pl.BlockSpec((H2, 1), lambda i: (0, 0))],
        out_specs=pl.BlockSpec((H2, tm), lambda i: (0, i)),
        out_shape=jax.ShapeDtypeStruct((H2, T), F32),
        compiler_params=_cparams("parallel"),
        name="mlstm_gates",
    )(w_gates_t, x, b_gate.reshape(H2, 1).astype(F32))


def _mlstm_body(q_ref, k_ref, v_ref, op_ref, li_ref, lf_ref, g_ref, out_ref,
                c_ref, n_ref, m_ref, *, scale):
    L = q_ref.shape[0]

    @pl.when(pl.program_id(2) == 0)
    def _():
        c_ref[...] = jnp.zeros_like(c_ref)
        n_ref[...] = jnp.zeros_like(n_ref)
        m_ref[...] = jnp.zeros_like(m_ref)

    q = q_ref[...]
    k = k_ref[...]
    v = v_ref[...]
    li = li_ref[...]
    lf = lf_ref[...]
    m_prev = m_ref[...]

    row = lax.broadcasted_iota(jnp.int32, (L, L), 0)
    col = lax.broadcasted_iota(jnp.int32, (L, L), 1)
    causal = col <= row
    eye = col == row

    def to_col(r):
        return jnp.sum(jnp.where(eye, jnp.broadcast_to(r, (L, L)), 0.0), axis=1, keepdims=True)

    b_col = jnp.sum(jnp.where(causal, jnp.broadcast_to(lf, (L, L)), 0.0), axis=1, keepdims=True)
    b_row = jnp.sum(jnp.where(eye, jnp.broadcast_to(b_col, (L, L)), 0.0), axis=0, keepdims=True)
    a_row = li - b_row
    a_b = jnp.broadcast_to(a_row, (L, L))
    mloc = jnp.maximum(m_prev, jnp.max(jnp.where(causal, a_b, NEG), axis=1, keepdims=True))
    dmat = jnp.exp(jnp.where(causal, a_b - mloc, NEG))
    inter = jnp.exp(m_prev - mloc)

    s = lax.dot_general(q, k, (((1,), (1,)), ((), ())), preferred_element_type=F32) * scale * dmat
    qc = jnp.dot(q, c_ref[...].astype(BF16), preferred_element_type=F32) * scale
    num = inter * qc + jnp.dot(s.astype(BF16), v, preferred_element_type=F32)
    qn = jnp.sum(q.astype(F32) * n_ref[...], axis=1, keepdims=True) * scale
    den = inter * qn + jnp.sum(s, axis=1, keepdims=True)
    m_t = b_col + mloc
    h = num / jnp.maximum(jnp.abs(den), jnp.exp(-m_t))

    m_last = jnp.maximum(m_prev, jnp.max(a_row, axis=1, keepdims=True))
    b_last = jnp.sum(lf, axis=1, keepdims=True)
    decay = jnp.exp(m_prev - m_last)
    w_col = to_col(jnp.exp(a_row - m_last))
    wk = w_col * k.astype(F32)
    c_ref[...] = decay * c_ref[...] + lax.dot_general(
        wk.astype(BF16), v, (((0,), (0,)), ((), ())), preferred_element_type=F32)
    n_ref[...] = decay * n_ref[...] + jnp.sum(wk, axis=0, keepdims=True)
    m_ref[...] = b_last + m_last

    hn = h * lax.rsqrt(jnp.mean(h * h, axis=1, keepdims=True) + NORM_EPS)
    gate = 1.0 / (1.0 + jnp.exp(-op_ref[...].astype(F32)))
    out_ref[...] = (hn * g_ref[...] * gate).astype(out_ref.dtype)


def _mlstm(proj, gates_t, norm_g, B, S, H, dk, dv):
    L = min(MLSTM_CHUNK, S)
    nc = S // L
    assert S % L == 0 and (2 * H * dk) % dv == 0
    v0 = (2 * H * dk) // dv
    gates3 = gates_t.reshape(2 * H, 1, B * S)
    return pl.pallas_call(
        functools.partial(_mlstm_body, scale=dk ** -0.5),
        grid=(B, H, nc),
        in_specs=[pl.BlockSpec((None, L, dk), lambda b, h, c: (b, c, h)),
                  pl.BlockSpec((None, L, dk), lambda b, h, c: (b, c, H + h)),
                  pl.BlockSpec((None, L, dv), lambda b, h, c: (b, c, v0 + h)),
                  pl.BlockSpec((None, L, dv), lambda b, h, c: (b, c, v0 + H + h)),
                  pl.BlockSpec((None, 1, L), lambda b, h, c: (h, 0, b * nc + c)),
                  pl.BlockSpec((None, 1, L), lambda b, h, c: (H + h, 0, b * nc + c)),
                  pl.BlockSpec((1, dv), lambda b, h, c: (0, h))],
        out_specs=pl.BlockSpec((None, L, dv), lambda b, h, c: (b, c, h)),
        out_shape=jax.ShapeDtypeStruct((B, S, H * dv), BF16),
        scratch_shapes=[pltpu.VMEM((dk, dv), F32), pltpu.VMEM((1, dk), F32), pltpu.VMEM((1, 1), F32)],
        compiler_params=_cparams("parallel", "parallel", "arbitrary"),
        name="mlstm_chunks",
    )(proj, proj, proj, proj, gates3, gates3, norm_g.reshape(1, H * dv).astype(F32))


def _slab_pitch(D):
    return D // LANES + SUBLANES


def _slab_load(ref, n_tok, D):
    pitch = _slab_pitch(D)
    return jnp.concatenate([ref[pl.ds(c, n_tok, stride=pitch), :] for c in range(D // LANES)], axis=1)


def _slab_store(ref, val, pad=True):
    n_tok, D = val.shape
    pitch = _slab_pitch(D)
    nrow = D // LANES
    for c in range(nrow):
        ref[pl.ds(c, n_tok, stride=pitch), :] = val[:, c * LANES:(c + 1) * LANES].astype(ref.dtype)
    if pad:
        for c in range(nrow, pitch):
            ref[pl.ds(c, n_tok, stride=pitch), :] = jnp.zeros((n_tok, LANES), ref.dtype)


def _ln_body(*refs, tm, slabs, weighted, alpha, router, outs):
    n_in = len(slabs)
    pos = n_in
    if weighted:
        aw = refs[pos][...]
        pos += 1
    g_ref, b_ref = refs[pos], refs[pos + 1]
    pos += 2
    D = g_ref.shape[1]
    vals = [_slab_load(r, tm, D) if sl else r[...].astype(F32) for r, sl in zip(refs[:n_in], slabs)]
    if router:
        wr_ref, br_ref = refs[pos], refs[pos + 1]
        pos += 2

    z = alpha * vals[0]
    for k, a in enumerate(vals[1:]):
        z = z + (a * aw[:, k:k + 1] if weighted else a)
    mu = jnp.mean(z, axis=1, keepdims=True)
    zc = z - mu
    var = jnp.mean(zc * zc, axis=1, keepdims=True)
    y = zc * lax.rsqrt(var + NORM_EPS) * g_ref[...] + b_ref[...]
    for kind in outs:
        if kind == "slab":
            _slab_store(refs[pos], y)
        else:
            refs[pos][...] = y.astype(refs[pos].dtype)
        pos += 1

    if router:
        ei_ref, ew_ref = refs[pos], refs[pos + 1]
        ng, epg = N_EXPERT_GROUPS, EXPERTS_PER_GROUP
        def split(a):
            hi = a.astype(BF16)
            return hi, (a - hi.astype(F32)).astype(BF16)

        def nt_dot(a, b):
            return lax.dot_general(a, b, (((1,), (1,)), ((), ())), preferred_element_type=F32)

        (w_hi, w_lo), (y_hi, y_lo) = split(wr_ref[...]), split(y)
        lt = nt_dot(w_hi, y_hi) + (nt_dot(w_hi, y_lo) + nt_dot(w_lo, y_hi)) + br_ref[...]
        ex = jnp.exp(lt - jnp.max(lt, axis=0, keepdims=True))
        p = ex / jnp.sum(ex, axis=0, keepdims=True)
        p0, p1, p2, p3 = [p[j * ng:(j + 1) * ng] for j in range(epg)]
        a, b = jnp.maximum(p0, p1), jnp.minimum(p0, p1)
        c, d = jnp.maximum(p2, p3), jnp.minimum(p2, p3)
        score = jnp.maximum(a, c) + jnp.maximum(jnp.minimum(a, c), jnp.maximum(b, d))
        gi = lax.broadcasted_iota(jnp.int32, score.shape, 0)
        best = jnp.min(jnp.where(score == jnp.max(score, axis=0, keepdims=True), gi, ng),
                       axis=0, keepdims=True)
        sel = gi == best
        v = [jnp.sum(jnp.where(sel, pj, 0.0), axis=0, keepdims=True) for pj in (p0, p1, p2, p3)]

        def first_max(vals):
            top = jnp.maximum(jnp.maximum(vals[0], vals[1]), jnp.maximum(vals[2], vals[3]))
            idx = jnp.where(vals[0] == top, 0, jnp.where(vals[1] == top, 1, jnp.where(vals[2] == top, 2, 3)))
            return top, idx

        v1, j1 = first_max(v)
        v2, j2 = first_max([jnp.where(j1 == j, -1.0, v[j]) for j in range(epg)])
        tot = v1 + v2
        ei_ref[...] = jnp.concatenate([best * epg + j1, best * epg + j2], axis=0).astype(jnp.int32)
        eye = (lax.broadcasted_iota(jnp.int32, (tm, tm), 0) == lax.broadcasted_iota(jnp.int32, (tm, tm), 1))
        cols = [jnp.sum(jnp.where(eye, jnp.broadcast_to(w, (tm, tm)), 0.0), axis=1, keepdims=True)
                for w in (v1 / tot, v2 / tot)]
        lane = lax.broadcasted_iota(jnp.int32, (tm, LANES), 1)
        ew_ref[...] = jnp.where(lane == 0, cols[0], jnp.where(lane == 1, cols[1], 0.0))


def _ln(T, D, inputs, g, b, alpha, outs, add_w=None, router=None, tm=256):
    tm = min(tm, T)
    pitch = _slab_pitch(D)

    def spec(is_slab, off):
        if is_slab:
            return pl.BlockSpec((tm * pitch, LANES), lambda i: (off + i, 0))
        return pl.BlockSpec((tm, D), lambda i: (off + i, 0))

    in_specs = [spec(sl, off) for _, sl, off in inputs]
    args = [a for a, _, _ in inputs]
    if add_w is not None:
        in_specs.append(pl.BlockSpec((tm, LANES), lambda i: (i, 0)))
        args.append(add_w)
    in_specs += [pl.BlockSpec((1, D), lambda i: (0, 0))] * 2
    args += [g.reshape(1, D).astype(F32), b.reshape(1, D).astype(F32)]
    out_shape, out_specs = [], []
    for kind in outs:
        if kind == "slab":
            out_shape.append(jax.ShapeDtypeStruct((T * pitch, LANES), F32))
        else:
            out_shape.append(jax.ShapeDtypeStruct((T, D), F32 if kind == "f32" else BF16))
        out_specs.append(spec(kind == "slab", 0))
    if router is not None:
        wr_t, br = router
        E = wr_t.shape[0]
        in_specs += [pl.BlockSpec((E, D), lambda i: (0, 0)), pl.BlockSpec((E, 1), lambda i: (0, 0))]
        args += [wr_t, br]
        out_shape += [jax.ShapeDtypeStruct((2, T), jnp.int32), jax.ShapeDtypeStruct((T, LANES), F32)]
        out_specs += [pl.BlockSpec((2, tm), lambda i: (0, i)), pl.BlockSpec((tm, LANES), lambda i: (i, 0))]
    return pl.pallas_call(
        functools.partial(_ln_body, tm=tm, slabs=tuple(sl for _, sl, _ in inputs),
                          weighted=add_w is not None, alpha=alpha,
                          router=router is not None, outs=tuple(outs)),
        grid=(T // tm,),
        in_specs=in_specs, out_specs=out_specs, out_shape=out_shape,
        compiler_params=_cparams("parallel"),
        name="deepnorm_ln_router" if router is not None else "deepnorm_ln",
    )(*args)


FFN_SUB = 256
FFN_VISIT_SUBS = 3
FFN_VMEM_LIMIT = 62 * 1024 * 1024


def _ffn_body(ve_ref, vrow_ref, vcnt_ref, nv_ref, ord_ref, tok_ref, h_hbm, hflat_hbm, wg_ref, wu_ref, wd_ref,
              y_hbm, gin_ref, gout0_ref, gout1_ref, xb_ref, acc_ref, pend_ref, sem_in, sem_out, *, D):
    v, j = pl.program_id(0), pl.program_id(1)
    n_fc = pl.num_programs(1)
    last_j = n_fc - 1
    pitch = _slab_pitch(D)
    n_data = D // LANES
    sub = FFN_SUB
    used = v < nv_ref[0]
    cnt, row0 = vcnt_ref[v], vrow_ref[v]

    def slab_row(ref, r):
        return ref.at[pl.ds(pl.multiple_of(r * pitch, SUBLANES), n_data)]

    def wait_rows(n, row_copy, rows_per_copy, vmem_ref, sem):
        @pl.when(n == sub)
        def _():
            whole = pl.ds(0, sub * rows_per_copy)
            pltpu.make_async_copy(hflat_hbm.at[whole], vmem_ref.at[whole], sem).wait()

        @pl.when(n < sub)
        def _():
            def body(r, carry):
                row_copy(r).wait()
                return carry
            lax.fori_loop(0, n, body, 0)

    def start_rows(n, row_copy):
        def pair(i, carry):
            row_copy(2 * i).start(priority=0)
            row_copy(2 * i + 1).start(priority=1)
            return carry
        lax.fori_loop(0, lax.shift_right_logical(n, 1), pair, 0)

        @pl.when((n & 1) == 1)
        def _():
            row_copy(n - 1).start(priority=0)

    def start_fetch(first_row, n):
        start_rows(n, lambda r: pltpu.make_async_copy(
            h_hbm.at[tok_ref[first_row + r], pl.ds(0, n_data)], slab_row(gin_ref, r), sem_in))

    def wait_fetch(n):
        wait_rows(n, lambda r: pltpu.make_async_copy(h_hbm.at[0, pl.ds(0, n_data)], slab_row(gin_ref, r), sem_in),
                  n_data, gin_ref, sem_in)

    gouts = (gout0_ref, gout1_ref)

    def out_row(b, r):
        return gouts[b].at[pl.ds(pl.multiple_of(r * pitch, SUBLANES), pitch)]

    def start_send(b, first_row, n):
        start_rows(n, lambda r: pltpu.make_async_copy(
            out_row(b, r), y_hbm.at[ord_ref[first_row + r]], sem_out.at[b]))
        pend_ref[b] = n

    def wait_sends(b):
        wait_rows(pend_ref[b], lambda r: pltpu.make_async_copy(out_row(b, r), y_hbm.at[0], sem_out.at[b]),
                  pitch, gouts[b], sem_out.at[b])
        pend_ref[b] = 0

    @pl.when((v == 0) & (j == 0))
    def _():
        gin_ref[...] = jnp.zeros_like(gin_ref)
        xb_ref[...] = jnp.zeros_like(xb_ref)
        for b in range(2):
            gouts[b][...] = jnp.zeros_like(gouts[b])
            pend_ref[b] = 0
        start_fetch(row0, jnp.minimum(cnt, sub))

    def ffn_chunk(x):
        g = jnp.dot(x, wg_ref[...].astype(BF16), preferred_element_type=F32)
        u = jnp.dot(x, wu_ref[...].astype(BF16), preferred_element_type=F32)
        hmid = g * (1.0 / (1.0 + jnp.exp(-g))) * u
        return jnp.dot(hmid.astype(BF16), wd_ref[...].astype(BF16), preferred_element_type=F32)

    n_subs = (cnt + sub - 1) // sub
    for k in range(1, FFN_VISIT_SUBS + 1):
        @pl.when(used & (j > 0) & (n_subs == k))
        def _():
            acc_ref[:k * sub, :] += ffn_chunk(xb_ref[:k * sub, :])

    for s in range(FFN_VISIT_SUBS):
        n_rows = jnp.minimum(cnt - s * sub, sub)
        active = used & (s * sub < cnt)
        rows = slice(s * sub, (s + 1) * sub)

        @pl.when(active & (j == 0))
        def _():
            wait_fetch(n_rows)
            xb_ref[rows, :] = _slab_load(gin_ref, sub, D).astype(BF16)
            if s + 1 < FFN_VISIT_SUBS:
                @pl.when((s + 1) * sub < cnt)
                def _():
                    start_fetch(row0 + (s + 1) * sub, jnp.minimum(cnt - (s + 1) * sub, sub))

        @pl.when(active & (j == 0))
        def _():
            acc_ref[rows, :] = ffn_chunk(xb_ref[rows, :])

        @pl.when(active & (j == last_j))
        def _():
            wait_sends(s % 2)
            _slab_store(gouts[s % 2], acc_ref[rows, :], pad=False)
            start_send(s % 2, row0 + s * sub, n_rows)

    @pl.when((j == jnp.minimum(1, last_j)) & (v + 1 < nv_ref[0]))
    def _():
        start_fetch(vrow_ref[v + 1], jnp.minimum(vcnt_ref[v + 1], sub))

    @pl.when((v == pl.num_programs(0) - 1) & (j == last_j))
    def _():
        wait_sends(0)
        wait_sends(1)


def _moe(h_slab, eidx, w_gate, w_up, w_down, layer, T, D):
    E, F = w_gate.shape[1], w_gate.shape[-1]
    A = 2 * T
    pitch = _slab_pitch(D)
    fc = 256 if F % 256 == 0 else F
    visit_rows = FFN_VISIT_SUBS * FFN_SUB
    n_visits = E + A // visit_rows

    e_flat = eidx.reshape(A)
    order = jnp.argsort(e_flat, stable=True).astype(jnp.int32)
    experts = jnp.arange(E, dtype=jnp.int32)
    counts = jnp.sum((e_flat[:, None] == experts[None, :]).astype(jnp.int32), axis=0)
    start = jnp.cumsum(counts) - counts
    visits_e = (counts + visit_rows - 1) // visit_rows
    visit_end = jnp.cumsum(visits_e)
    n_used = visit_end[-1:].astype(jnp.int32)
    vid = jnp.minimum(jnp.arange(n_visits, dtype=jnp.int32), n_used[0] - 1)
    visit_expert = jnp.sum((vid[:, None] >= visit_end[None, :]).astype(jnp.int32), axis=1).astype(jnp.int32)
    k_in_expert = vid - (visit_end - visits_e)[visit_expert]
    visit_row0 = (start[visit_expert] + k_in_expert * visit_rows).astype(jnp.int32)
    visit_cnt = jnp.minimum(counts[visit_expert] - k_in_expert * visit_rows, visit_rows).astype(jnp.int32)

    n_fc = F // fc

    def chunk(v, j, nv):
        return jnp.where(v < nv[0], j, n_fc - 1)

    def w_map(v, j, ve, vrow, vcnt, nv, ordr, tok):
        return (layer, ve[v], 0, chunk(v, j, nv))

    out = pl.pallas_call(
        functools.partial(_ffn_body, D=D),
        grid_spec=pltpu.PrefetchScalarGridSpec(
            num_scalar_prefetch=6, grid=(n_visits, n_fc),
            in_specs=[pl.BlockSpec(memory_space=pl.ANY),
                      pl.BlockSpec(memory_space=pl.ANY),
                      pl.BlockSpec((None, None, D, fc), w_map),
                      pl.BlockSpec((None, None, D, fc), w_map),
                      pl.BlockSpec((None, None, fc, D),
                                   lambda v, j, ve, vrow, vcnt, nv, ordr, tok: (layer, ve[v], chunk(v, j, nv), 0))],
            out_specs=pl.BlockSpec(memory_space=pl.ANY),
            scratch_shapes=[pltpu.VMEM((FFN_SUB * pitch, LANES), F32), pltpu.VMEM((FFN_SUB * pitch, LANES), F32),
                            pltpu.VMEM((FFN_SUB * pitch, LANES), F32),
                            pltpu.VMEM((visit_rows, D), BF16), pltpu.VMEM((visit_rows, D), F32),
                            pltpu.SMEM((2,), jnp.int32), pltpu.SemaphoreType.DMA(()),
                            pltpu.SemaphoreType.DMA((2,))]),
        out_shape=jax.ShapeDtypeStruct((A, pitch, LANES), F32),
        compiler_params=pltpu.CompilerParams(dimension_semantics=("arbitrary", "arbitrary"),
                                             vmem_limit_bytes=FFN_VMEM_LIMIT),
        name="expert_ffn",
    )(visit_expert, visit_row0, visit_cnt, n_used, order, jnp.where(order >= T, order - T, order),
      h_slab.reshape(T, pitch, LANES), h_slab, w_gate, w_up, w_down)
    return out.reshape(A * pitch, LANES)


def _proj_heads_body(x_ref, w_ref, ca_ref, sb_ref, sc_ref, o_ref, acc_ref, *, d, hb, rope):
    acc = jnp.dot(x_ref[...], w_ref[...].astype(BF16), preferred_element_type=F32)
    tm = x_ref.shape[0]
    n = tm // d
    half = ROPE_DIM // 2
    if d > 1:
        for hh in range(hb):
            acc_ref[hh * tm:(hh + 1) * tm, :] = acc[:, hh * HEAD_DIM:(hh + 1) * HEAD_DIM]
    for r in range(d):
        if rope:
            rows = pl.ds(r, n, stride=d) if d > 1 else pl.ds(0, n)
            ca, sb, sc = ca_ref[rows, :], sb_ref[rows, :], sc_ref[rows, :]
        for hh in range(hb):
            if d > 1:
                val = acc_ref[pl.ds(hh * tm + r, n, stride=d), :]
            else:
                val = acc[:, hh * HEAD_DIM:(hh + 1) * HEAD_DIM]
            if rope:
                val = (val * ca + pltpu.roll(val, half, 1) * sb
                       + pltpu.roll(val, HEAD_DIM - half, 1) * sc)
            o_ref[hh, r] = val.astype(o_ref.dtype)


def _proj_heads(x, w, layer, col0, n_heads, d, B, S, tables, rope, tm=1024, hb=4):
    M, K = x.shape
    tm = min(tm, S)
    tn = hb * HEAD_DIM
    assert S % tm == 0 and n_heads % hb == 0 and col0 % tn == 0 and (tm // d) % 16 == 0
    mt = S // tm
    c0 = col0 // tn
    tab_spec = pl.BlockSpec((tm, HEAD_DIM), lambda i, j: (i % mt, 0))
    return pl.pallas_call(
        functools.partial(_proj_heads_body, d=d, hb=hb, rope=rope),
        grid=(M // tm, n_heads // hb),
        in_specs=[pl.BlockSpec((tm, K), lambda i, j: (i, 0)),
                  pl.BlockSpec((None, K, tn), lambda i, j: (layer, 0, c0 + j)),
                  tab_spec, tab_spec, tab_spec],
        out_specs=pl.BlockSpec((None, hb, d, tm // d, HEAD_DIM), lambda i, j: (i // mt, j, 0, i % mt, 0)),
        out_shape=jax.ShapeDtypeStruct((B, n_heads, d, S // d, HEAD_DIM), BF16),
        scratch_shapes=[pltpu.VMEM((hb * tm, HEAD_DIM), F32)],
        compiler_params=_cparams("parallel", "parallel"),
        name="attn_proj_d%d" % d,
    )(x, w, *tables)


def _rope_tables(S, scale):
    half = ROPE_DIM // 2
    inv_freq = ROPE_THETA ** (-jnp.arange(half, dtype=F32) / half)
    ang = jnp.arange(S).astype(F32)[:, None] * inv_freq[None, :]
    cos, sin = jnp.cos(ang), jnp.sin(ang)
    zeros = jnp.zeros((S, HEAD_DIM - ROPE_DIM), F32)
    ca = jnp.concatenate([cos, cos, jnp.ones_like(zeros)], axis=1)
    sb = jnp.concatenate([jnp.zeros_like(sin), sin, zeros], axis=1)
    sc = jnp.concatenate([-sin, jnp.zeros_like(sin), zeros], axis=1)
    return ca * scale, sb * scale, sc * scale


def _attn_body(q_ref, kc_ref, kp_ref, vc_ref, vp_ref, o_ref, lse_ref, kbuf, vbuf):
    SB, LB, _ = q_ref.shape
    blk = ATT_BLOCK
    n = pl.program_id(1)
    kbuf[:, :blk] = kp_ref[...]
    kbuf[:, blk:] = kc_ref[...]
    vbuf[:, :blk] = vp_ref[...]
    vbuf[:, blk:] = vc_ref[...]

    row = lax.broadcasted_iota(jnp.int32, (blk, 2 * blk), 0)
    col = lax.broadcasted_iota(jnp.int32, (blk, 2 * blk), 1)
    band = (col >= row) & (col <= row + blk)
    band_first = band & ((col >= blk) | (n > 0))
    eye = (lax.broadcasted_iota(jnp.int32, (blk, blk), 0) == lax.broadcasted_iota(jnp.int32, (blk, blk), 1))

    for sb in range(SB):
        for qi in range(LB // blk):
            q = q_ref[sb, qi * blk:(qi + 1) * blk, :]
            kk = kbuf[sb, qi * blk:(qi + 2) * blk, :]
            vv = vbuf[sb, qi * blk:(qi + 2) * blk, :]
            s = lax.dot_general(q, kk, (((1,), (1,)), ((), ())), preferred_element_type=F32)
            s = jnp.where(band_first if qi == 0 else band, s, NEG)
            m = jnp.max(s, axis=1, keepdims=True)
            p = jnp.exp(s - m)
            l = jnp.sum(p, axis=1, keepdims=True)
            o = jnp.dot(p.astype(BF16), vv, preferred_element_type=F32) / l
            o_ref[sb, qi * blk:(qi + 1) * blk, :] = o.astype(o_ref.dtype)
            lse = m + jnp.log(l)
            lse_ref[sb, qi:qi + 1, :] = jnp.sum(
                jnp.where(eye, jnp.broadcast_to(lse, (blk, blk)), 0.0), axis=0, keepdims=True)


def _band_attention(q, k, v, rows_per_step=2048):
    NS, L, Dh = q.shape
    blk = ATT_BLOCK
    LB = min(L, rows_per_step)
    SB = max(1, rows_per_step // LB)
    assert L % LB == 0 and NS % SB == 0 and LB % blk == 0
    nlb = LB // blk
    cur = pl.BlockSpec((SB, LB, Dh), lambda s, n: (s, n, 0))
    prev = pl.BlockSpec((SB, blk, Dh), lambda s, n: (s, jnp.maximum(n * nlb - 1, 0), 0))
    return pl.pallas_call(
        _attn_body,
        grid=(NS // SB, L // LB),
        in_specs=[cur, cur, prev, cur, prev],
        out_specs=[cur, pl.BlockSpec((SB, nlb, blk), lambda s, n: (s, n, 0))],
        out_shape=[jax.ShapeDtypeStruct((NS, L, Dh), BF16),
                   jax.ShapeDtypeStruct((NS, L // blk, blk), F32)],
        scratch_shapes=[pltpu.VMEM((SB, LB + blk, Dh), BF16), pltpu.VMEM((SB, LB + blk, Dh), BF16)],
        compiler_params=_cparams("parallel", "arbitrary"),
        name="band_attention",
    )(q, k, k, v, v)


def _combine_body(*refs, dils):
    G = len(dils)
    o_refs, l_refs = refs[:G], refs[G:2 * G]
    out_ref, acc_ref, wcol_ref = refs[2 * G:]
    S = out_ref.shape[0]
    blk = LANES
    eye = (lax.broadcasted_iota(jnp.int32, (blk, blk), 0) == lax.broadcasted_iota(jnp.int32, (blk, blk), 1))

    lses = [l[...] for l in l_refs]
    mx = functools.reduce(jnp.maximum, lses)
    es = [jnp.exp(l - mx) for l in lses]
    tot = functools.reduce(lambda a, b: a + b, es)
    for g in range(G):
        mix = es[g] / tot
        for tb in range(S // blk):
            seg = jnp.broadcast_to(mix[:, tb * blk:(tb + 1) * blk], (blk, blk))
            col = jnp.sum(jnp.where(eye, seg, 0.0), axis=1, keepdims=True)
            wcol_ref[g * S + tb * blk:g * S + (tb + 1) * blk, :] = jnp.broadcast_to(col, (blk, blk))
    for g, d in enumerate(dils):
        n = S // d
        for r in range(d):
            rows = pl.ds(r, n, stride=d) if d > 1 else pl.ds(0, n)
            wrows = pl.ds(g * S + r, n, stride=d) if d > 1 else pl.ds(g * S, n)
            contrib = o_refs[g][r].astype(F32) * wcol_ref[wrows, :]
            if g == 0:
                acc_ref[rows, :] = contrib
            else:
                acc_ref[rows, :] += contrib
    out_ref[...] = acc_ref[...].astype(out_ref.dtype)


def _combine(os_, lses, B, S, n_heads):
    dils = tuple(d for _, d in ATT_GROUPS)
    in_specs = [pl.BlockSpec((None, None, d, S // d, HEAD_DIM), lambda b, h: (b, h, 0, 0, 0)) for d in dils]
    in_specs += [pl.BlockSpec((None, None, 1, S), lambda b, h: (b, h, 0, 0))] * len(dils)
    return pl.pallas_call(
        functools.partial(_combine_body, dils=dils),
        grid=(B, n_heads),
        in_specs=in_specs,
        out_specs=pl.BlockSpec((None, S, HEAD_DIM), lambda b, h: (b, 0, h)),
        out_shape=jax.ShapeDtypeStruct((B, S, n_heads * HEAD_DIM), BF16),
        scratch_shapes=[pltpu.VMEM((S, HEAD_DIM), F32), pltpu.VMEM((len(dils) * S, LANES), F32)],
        compiler_params=_cparams("parallel", "parallel"),
        name="group_combine",
    )(*os_, *lses)


def kernel(x, mlstm_w_in, mlstm_b_gate, mlstm_norm_g, mlstm_w_out, att_w_q, att_w_kv_shared, att_w_o,
           router_w, router_b, expert_w_gate, expert_w_up, expert_w_down, ln_g, ln_b):
    B, S, D = x.shape
    T = B * S
    depth = ln_g.shape[0]
    n_a = mlstm_w_in.shape[0]
    alpha = (2.0 * depth) ** 0.25

    Hm = mlstm_b_gate.shape[-1] // 2
    dv = mlstm_w_out.shape[1] // Hm
    dk = (mlstm_w_in.shape[-1] - 2 * Hm - 2 * Hm * dv) // (2 * Hm)
    n_main = 2 * Hm * dk + 2 * Hm * dv

    AH = att_w_o.shape[1] // HEAD_DIM
    G = len(ATT_GROUPS)
    AW = G * AH * HEAD_DIM
    E = router_w.shape[1]
    assert E == N_EXPERT_GROUPS * EXPERTS_PER_GROUP

    perm = (jnp.arange(E) % N_EXPERT_GROUPS) * EXPERTS_PER_GROUP + jnp.arange(E) // N_EXPERT_GROUPS
    router = (router_w.T[perm].astype(F32), router_b[perm].reshape(E, 1).astype(F32))

    h = x.reshape(T, D)
    h_b = h.astype(BF16)
    kv = None
    for layer in range(depth):
        if layer < n_a:
            w_in_t = jnp.swapaxes(mlstm_w_in, 1, 2)
            proj = _mm(h_b, w_in_t, layer, 0, n_main, BF16, w_transposed=True)
            gates_t = _mlstm_gates(h_b, w_in_t[layer, n_main:], mlstm_b_gate[layer])
            hm = _mlstm(proj.reshape(B, S, n_main), gates_t, mlstm_norm_g[layer], B, S, Hm, dk, dv)
            mix = _mm(hm.reshape(T, Hm * dv), mlstm_w_out, layer, 0, D, BF16)
        else:
            bl = layer - n_a
            k_tabs = _rope_tables(S, 1.0)
            q_tabs = _rope_tables(S, HEAD_DIM ** -0.5)
            w_kv = att_w_kv_shared.reshape(1, D, 2 * AW)
            os_, lses = [], []
            for g, (window, d) in enumerate(ATT_GROUPS):
                assert window // d == ATT_BLOCK and (S // d) % ATT_BLOCK == 0
                if kv is None or len(kv) <= g:
                    kv = (kv or []) + [(
                        _proj_heads(h_b, w_kv, 0, g * AH * HEAD_DIM, AH, d, B, S, k_tabs, True),
                        _proj_heads(h_b, w_kv, 0, AW + g * AH * HEAD_DIM, AH, d, B, S, k_tabs, False))]
                kg, vg = kv[g]
                qg = _proj_heads(h_b, att_w_q, bl, g * AH * HEAD_DIM, AH, d, B, S, q_tabs, True)
                L = S // d
                ns = B * AH * d
                o, lse = _band_attention(qg.reshape(ns, L, HEAD_DIM), kg.reshape(ns, L, HEAD_DIM),
                                         vg.reshape(ns, L, HEAD_DIM))
                os_.append(o.reshape(B, AH, d, L, HEAD_DIM))
                lses.append(lse.reshape(B, AH, d, L).transpose(0, 1, 3, 2).reshape(B, AH, 1, S))
            ob = _combine(os_, lses, B, S, AH)
            mix = _mm(ob.reshape(T, AH * HEAD_DIM), att_w_o, bl, 0, D, BF16)

        tm_ln = min(256, T)
        h1_slab, eidx, ew = _ln(T, D, [(h, False, 0), (mix, False, 0)], ln_g[layer, 0], ln_b[layer, 0],
                                alpha, ("slab",), router=router, tm=tm_ln)
        y2 = _moe(h1_slab, eidx, expert_w_gate, expert_w_up, expert_w_down, layer, T, D)
        last = layer == depth - 1
        outs = _ln(T, D, [(h1_slab, True, 0), (y2, True, 0), (y2, True, T // tm_ln)],
                   ln_g[layer, 1], ln_b[layer, 1], alpha, ("f32",) if last else ("f32", "bf16"),
                   add_w=ew, tm=tm_ln)
        h = outs[0]
        h_b = None if last else outs[1]
    return h.reshape(B, S, D)
```
